```python
import jax, jax.numpy as jnp
from jax import lax
import numpy as np

D_MODEL = 1024
BATCH = 1
SEQ = 16384
DEPTH = 2

GRID_W = 64
CTX_LEN = 256
N_MIXERS = 2
N_HEADS = 8
N_KV_HEADS = 2
HEAD_DIM = 128
KV_GROUP = N_HEADS // N_KV_HEADS
ROPE_THETA = 10000.0
Q_BLOCK = 128
POOL_WINDOWS = (2, 4, 8, 16)
N_POOL_GROUPS = len(POOL_WINDOWS)
POOL_GROUP = D_MODEL // N_POOL_GROUPS
N_EXPERTS = 32
TOP_K = 4
D_FF_EXPERT = D_MODEL
SWIGLU_LIMIT = 7.0
SWIGLU_ALPHA = 1.702
NORM_EPS = 1e-6
N_MOD = 6

kernel_name = "hybrid_attn_pool_moe_dit_block"


def _rmsnorm(x, g):
    x32 = x.astype(jnp.float32)
    y = x32 * lax.rsqrt(jnp.mean(x32 * x32, axis=-1, keepdims=True) + NORM_EPS)
    return (y * g.astype(jnp.float32)).astype(x.dtype)


def _modulate(h, shift, scale):
    return h * (1 + scale) + shift


def _rope_1d(x, pos):
    half = x.shape[-1] // 2
    inv_freq = ROPE_THETA ** (-jnp.arange(half, dtype=jnp.float32) / half)
    ang = pos.astype(jnp.float32)[:, None] * inv_freq[None, :]
    cos = jnp.cos(ang)[:, None, :]
    sin = jnp.sin(ang)[:, None, :]
    x32 = x.astype(jnp.float32)
    x1, x2 = x32[..., :half], x32[..., half:]
    return jnp.concatenate([x1 * cos - x2 * sin, x2 * cos + x1 * sin], axis=-1).astype(x.dtype)


def _rope_2d(x, row, col):
    rd = HEAD_DIM // 2
    return jnp.concatenate([_rope_1d(x[..., :rd], row), _rope_1d(x[..., rd:], col)], axis=-1)


def _attend(q, k, v):
    s = jnp.einsum('bqkgd,btkd->bkgqt', q, k, preferred_element_type=jnp.float32) * (HEAD_DIM ** -0.5)
    p = jax.nn.softmax(s, axis=-1)
    o = jnp.einsum('bkgqt,btkd->bqkgd', p, v.astype(jnp.float32))
    return o.astype(v.dtype)


def _attention_mixer(h_lat, h_ctx, w_qkv, q_g, k_g, w_o, row, col, need_ctx_out):
    B, S, _ = h_lat.shape
    C = h_ctx.shape[1]

    def project(h):
        qkv = h @ w_qkv
        q, k, v = jnp.split(qkv, [N_HEADS * HEAD_DIM, (N_HEADS + N_KV_HEADS) * HEAD_DIM], axis=-1)
        q = _rmsnorm(q.reshape(B, -1, N_HEADS, HEAD_DIM), q_g)
        k = _rmsnorm(k.reshape(B, -1, N_KV_HEADS, HEAD_DIM), k_g)
        v = v.reshape(B, -1, N_KV_HEADS, HEAD_DIM)
        return q, k, v

    q_l, k_l, v_l = project(h_lat)
    q_c, k_c, v_c = project(h_ctx)
    q_l = _rope_2d(q_l, row, col)
    k_l = _rope_2d(k_l, row, col)
    k_all = jnp.concatenate([k_l, k_c], axis=1)
    v_all = jnp.concatenate([v_l, v_c], axis=1)

    n_blk = S // Q_BLOCK
    q_blocks = q_l.reshape(B, n_blk, Q_BLOCK, N_KV_HEADS, KV_GROUP, HEAD_DIM).transpose(1, 0, 2, 3, 4, 5)
    o_l = lax.map(lambda qb: _attend(qb, k_all, v_all), q_blocks)
    o_l = o_l.transpose(1, 0, 2, 3, 4, 5).reshape(B, S, N_HEADS * HEAD_DIM) @ w_o
    if need_ctx_out:
        o_c = _attend(q_c.reshape(B, C, N_KV_HEADS, KV_GROUP, HEAD_DIM), k_c, v_c)
        o_c = o_c.reshape(B, C, N_HEADS * HEAD_DIM) @ w_o
    else:
        o_c = None
    return o_l, o_c


def _pool_mixer(h, w_pool, scale):
    B, S, D = h.shape
    h32 = h.astype(jnp.float32)
    cs = jnp.concatenate([jnp.zeros((B, 1, D), jnp.float32), lax.cumsum(h32, axis=1)], axis=1)
    t = jnp.arange(S)
    outs = []
    for gi, w in enumerate(POOL_WINDOWS):
        lo = jnp.clip(t - w // 2, 0, S)
        hi = jnp.clip(t - w // 2 + w, 0, S)
        sl = slice(gi * POOL_GROUP, (gi + 1) * POOL_GROUP)
        csg = cs[:, :, sl]
        mean = (csg[:, hi, :] - csg[:, lo, :]) / (hi - lo).astype(jnp.float32)[None, :, None]
        outs.append(mean - h32[:, :, sl])
    pooled = jnp.concatenate(outs, axis=-1).astype(h.dtype).reshape(B, S, N_POOL_GROUPS, POOL_GROUP)
    y = jnp.einsum('bsgc,gcd->bsgd', pooled, w_pool).reshape(B, S, D)
    return y * scale


def _moe(h, router_w, router_b, w_gu, b_gu, w_down, b_down):
    logits = (h @ router_w + router_b).astype(jnp.float32)
    top_val, top_idx = lax.top_k(logits, TOP_K)
    top_w = jax.nn.softmax(top_val, axis=-1)
    gates = jnp.sum(jax.nn.one_hot(top_idx, N_EXPERTS, dtype=jnp.float32) * top_w[..., None], axis=1)

    def expert(acc, p):
        wgu, bgu, wd, bd, g = p
        gu = h @ wgu + bgu
        gate = jnp.minimum(gu[:, :D_FF_EXPERT], SWIGLU_LIMIT)
        up = jnp.clip(gu[:, D_FF_EXPERT:], -SWIGLU_LIMIT, SWIGLU_LIMIT)
        glu = gate * jax.nn.sigmoid(SWIGLU_ALPHA * gate)
        y = ((up + 1) * glu) @ wd + bd
        return acc + g[:, None].astype(acc.dtype) * y, None

    out, _ = lax.scan(expert, jnp.zeros_like(h), (w_gu, b_gu, w_down, b_down, gates.T))
    return out


def setup_inputs(seed: int = 0) -> dict:
    key = jax.random.key(seed)
    ks = jax.random.split(key, 24)
    f32 = jnp.float32

    def nrm(k, shape, s):
        return jax.random.normal(k, shape, f32) * s

    n_attn = len([i for i in range(DEPTH) if i % N_MIXERS == 0])
    n_pool = DEPTH - n_attn
    D, E, F = D_MODEL, N_EXPERTS, D_FF_EXPERT
    qkv_w = (N_HEADS + 2 * N_KV_HEADS) * HEAD_DIM
    return {
        "x": nrm(ks[0], (BATCH, SEQ, D), 1.0),
        "c": nrm(ks[1], (BATCH, D), 1.0),
        "ctx": nrm(ks[2], (BATCH, CTX_LEN, D), 1.0),
        "c_ctx": nrm(ks[3], (D,), 1.0),
        "ada_w": nrm(ks[4], (DEPTH, D, N_MOD * D), 0.5 * D ** -0.5),
        "ada_b": nrm(ks[5], (DEPTH, N_MOD * D), 0.01),
        "norm_mix": 1.0 + nrm(ks[6], (DEPTH, D), 0.05),
        "norm_ffn": 1.0 + nrm(ks[7], (DEPTH, D), 0.05),
        "attn_w_qkv": nrm(ks[8], (n_attn, D, qkv_w), D ** -0.5),
        "attn_q_norm": 1.0 + nrm(ks[9], (n_attn, HEAD_DIM), 0.05),
        "attn_k_norm": 1.0 + nrm(ks[10], (n_attn, HEAD_DIM), 0.05),
        "attn_w_o": nrm(ks[11], (n_attn, N_HEADS * HEAD_DIM, D), (N_HEADS * HEAD_DIM) ** -0.5),
        "pool_w": nrm(ks[12], (n_pool, N_POOL_GROUPS, POOL_GROUP, POOL_GROUP), POOL_GROUP ** -0.5),
        "pool_scale": 1.0 + nrm(ks[13], (n_pool, D), 0.1),
        "moe_router_w": nrm(ks[14], (DEPTH, D, E), D ** -0.5),
        "moe_router_b": nrm(ks[15], (DEPTH, E), 0.01),
        "moe_w_gu": nrm(ks[16], (DEPTH, E, D, 2 * F), D ** -0.5),
        "moe_b_gu": nrm(ks[17], (DEPTH, E, 2 * F), 0.01),
        "moe_w_down": nrm(ks[18], (DEPTH, E, F, D), F ** -0.5),
        "moe_b_down": nrm(ks[19], (DEPTH, E, D), 0.01),
        "final_norm": 1.0 + nrm(ks[20], (D,), 0.05),
    }


def reference(x, c, ctx, c_ctx, ada_w, ada_b, norm_mix, norm_ffn, attn_w_qkv, attn_q_norm, attn_k_norm,
              attn_w_o, pool_w, pool_scale, moe_router_w, moe_router_b, moe_w_gu, moe_b_gu, moe_w_down,
              moe_b_down, final_norm):
    B, S, D = x.shape
    C = ctx.shape[1]
    ROWS = S // GRID_W
    row = jnp.repeat(jnp.arange(ROWS), GRID_W)
    col = jnp.tile(jnp.arange(GRID_W), ROWS)

    x_lat, x_ctx = x, ctx
    for i in range(DEPTH):
        kind = i % N_MIXERS
        mix_idx = i // N_MIXERS
        ctx_needed = any((j % N_MIXERS) == 0 for j in range(i + 1, DEPTH))

        m_l = jax.nn.silu(c) @ ada_w[i] + ada_b[i]
        m_c = jax.nn.silu(c_ctx) @ ada_w[i] + ada_b[i]
        sh_l, sc_l, g_l, shf_l, scf_l, gf_l = jnp.split(m_l[:, None, :], N_MOD, axis=-1)
        sh_c, sc_c, g_c, shf_c, scf_c, gf_c = jnp.split(m_c[None, None, :], N_MOD, axis=-1)

        h_l = _modulate(_rmsnorm(x_lat, norm_mix[i]), sh_l, sc_l)
        h_c = _modulate(_rmsnorm(x_ctx, norm_mix[i]), sh_c, sc_c)
        if kind == 0:
            y_l, y_c = _attention_mixer(h_l, h_c, attn_w_qkv[mix_idx], attn_q_norm[mix_idx],
                                        attn_k_norm[mix_idx], attn_w_o[mix_idx], row, col, ctx_needed)
        else:
            y_l = _pool_mixer(h_l, pool_w[mix_idx], pool_scale[mix_idx])
            y_c = _pool_mixer(h_c, pool_w[mix_idx], pool_scale[mix_idx]) if ctx_needed else None
        x_lat = x_lat + g_l * y_l
        if ctx_needed:
            x_ctx = x_ctx + g_c * y_c

        hf_l = _modulate(_rmsnorm(x_lat, norm_ffn[i]), shf_l, scf_l).reshape(B * S, D)
        moe_args = (moe_router_w[i], moe_router_b[i], moe_w_gu[i], moe_b_gu[i], moe_w_down[i], moe_b_down[i])
        if ctx_needed:
            hf_c = _modulate(_rmsnorm(x_ctx, norm_ffn[i]), shf_c, scf_c).reshape(B * C, D)
            f = _moe(jnp.concatenate([hf_l, hf_c], axis=0), *moe_args)
            x_lat = x_lat + gf_l * f[:B * S].reshape(B, S, D)
            x_ctx = x_ctx + gf_c * f[B * S:].reshape(B, C, D)
        else:
            f = _moe(hf_l, *moe_args)
            x_lat = x_lat + gf_l * f.reshape(B, S, D)

    return _rmsnorm(x_lat, final_norm)
```

```python
import functools

import numpy as np
import jax
import jax.numpy as jnp
from jax import lax
from jax.experimental import pallas as pl
from jax.experimental.pallas import tpu as pltpu

D_MODEL = 1024
GRID_W = 64
N_HEADS = 8
N_KV_HEADS = 2
KV_GROUP = N_HEADS // N_KV_HEADS
HEAD_DIM = 128
ROPE_THETA = 10000.0
POOL_WINDOWS = (2, 4, 8, 16)
POOL_GROUP = D_MODEL // len(POOL_WINDOWS)
POOL_HALO = 8
N_EXPERTS = 32
TOP_K = 4
D_FF = D_MODEL
SWIGLU_LIMIT = 7.0
SWIGLU_ALPHA = 1.702
NORM_EPS = 1e-6
N_MOD = 6

LANES = 128
SUBLANES = 8
D_CHUNKS = D_MODEL // LANES
TM = 256
TK = 512
VMEM_LIMIT = 48 * 1024 * 1024
LOG2E = 1.4426950408889634
NEG_BIG = -1e30

_HI = lax.Precision.HIGHEST


def _cparams(sem):
    return pltpu.CompilerParams(dimension_semantics=sem, vmem_limit_bytes=VMEM_LIMIT)


def _rms(x, eps=NORM_EPS):
    return x * lax.rsqrt(jnp.mean(x * x, axis=-1, keepdims=True) + eps)


def _mod_kernel(cv_ref, w_ref, b_ref, o_ref):
    cv = cv_ref[...]
    s = cv / (1.0 + jnp.exp(-cv))
    o_ref[0] = jnp.dot(s, w_ref[0], precision=_HI, preferred_element_type=jnp.float32) + b_ref[0]


def _modulation(cvec, ada_w, ada_b):
    depth = ada_w.shape[0]
    nblk = 4
    bw = N_MOD * D_MODEL // nblk
    return pl.pallas_call(
        _mod_kernel,
        grid=(depth, nblk),
        in_specs=[
            pl.BlockSpec((SUBLANES, D_MODEL), lambda l, j: (0, 0)),
            pl.BlockSpec((1, D_MODEL, bw), lambda l, j: (l, 0, j)),
            pl.BlockSpec((1, 1, bw), lambda l, j: (l, 0, j)),
        ],
        out_specs=pl.BlockSpec((1, SUBLANES, bw), lambda l, j: (l, 0, j)),
        out_shape=jax.ShapeDtypeStruct((depth, SUBLANES, N_MOD * D_MODEL), jnp.float32),
        compiler_params=_cparams(("arbitrary", "arbitrary")),
        name="modulation",
    )(cvec, ada_w, ada_b.reshape(depth, 1, N_MOD * D_MODEL))


def _rope(x, cos, sin_signed):
    lane = lax.broadcasted_iota(jnp.int32, x.shape, 1)
    partner = jnp.where((lane % 64) < 32, pltpu.roll(x, LANES - 32, 1), pltpu.roll(x, 32, 1))
    return x * cos + partner * sin_signed


def _qkv_kernel(n_lat_tiles, x_ref, ctx_ref, vec_ref, w_ref, qg_ref, kg_ref, cos_ref, sin_ref,
                q_ref, k_ref, v_ref):
    i = pl.program_id(0)
    is_ctx = i >= n_lat_tiles
    xt = jnp.where(is_ctx, ctx_ref[...], x_ref[...])
    sh = jnp.where(is_ctx, vec_ref[2:3, :], vec_ref[0:1, :])
    sc = jnp.where(is_ctx, vec_ref[3:4, :], vec_ref[1:2, :])
    h = _rms(xt) * vec_ref[4:5, :] * (1.0 + sc) + sh
    qkv = jnp.dot(h.astype(jnp.bfloat16), w_ref[...], preferred_element_type=jnp.float32)
    cos = cos_ref[...]
    sin = sin_ref[...]
    q_scale = (HEAD_DIM ** -0.5) * LOG2E
    for hd in range(N_HEADS):
        qh = _rms(qkv[:, hd * HEAD_DIM:(hd + 1) * HEAD_DIM]) * qg_ref[...]
        q_ref[hd] = (_rope(qh, cos, sin) * q_scale).astype(jnp.bfloat16)
    for kv in range(N_KV_HEADS):
        c0 = (N_HEADS + kv) * HEAD_DIM
        kh = _rms(qkv[:, c0:c0 + HEAD_DIM]) * kg_ref[...]
        k_ref[kv] = _rope(kh, cos, sin).astype(jnp.bfloat16)
        c1 = (N_HEADS + N_KV_HEADS + kv) * HEAD_DIM
        v_ref[kv] = qkv[:, c1:c1 + HEAD_DIM].astype(jnp.bfloat16)


def _qkv(x, ctx, vecs, w_qkv, q_g, k_g, cos_t, sin_t):
    S, C = x.shape[0], ctx.shape[0]
    n_lat = S // TM
    n_ctx = C // TM
    T = S + C
    last = n_lat - 1
    qkv_w = w_qkv.shape[1]
    out_shapes = (
        jax.ShapeDtypeStruct((N_HEADS, T, HEAD_DIM), jnp.bfloat16),
        jax.ShapeDtypeStruct((N_KV_HEADS, T, HEAD_DIM), jnp.bfloat16),
        jax.ShapeDtypeStruct((N_KV_HEADS, T, HEAD_DIM), jnp.bfloat16),
    )
    return pl.pallas_call(
        functools.partial(_qkv_kernel, n_lat),
        grid=(n_lat + n_ctx,),
        in_specs=[
            pl.BlockSpec((TM, D_MODEL), lambda i: (jnp.minimum(i, last), 0)),
            pl.BlockSpec((TM, D_MODEL), lambda i: (jnp.maximum(i - n_lat, 0), 0)),
            pl.BlockSpec((SUBLANES, D_MODEL), lambda i: (0, 0)),
            pl.BlockSpec((D_MODEL, qkv_w), lambda i: (0, 0)),
            pl.BlockSpec((1, HEAD_DIM), lambda i: (0, 0)),
            pl.BlockSpec((1, HEAD_DIM), lambda i: (0, 0)),
            pl.BlockSpec((TM, HEAD_DIM), lambda i: (i, 0)),
            pl.BlockSpec((TM, HEAD_DIM), lambda i: (i, 0)),
        ],
        out_specs=(
            pl.BlockSpec((N_HEADS, TM, HEAD_DIM), lambda i: (0, i, 0)),
            pl.BlockSpec((N_KV_HEADS, TM, HEAD_DIM), lambda i: (0, i, 0)),
            pl.BlockSpec((N_KV_HEADS, TM, HEAD_DIM), lambda i: (0, i, 0)),
        ),
        out_shape=out_shapes,
        compiler_params=_cparams(("arbitrary",)),
        name="qkv_proj",
    )(x, ctx, vecs, w_qkv, q_g, k_g, cos_t, sin_t)


def _attn_kernel(n_full, tail, q_ref, k_ref, v_ref, o_ref, m_ref, l_ref, acc_ref):
    rows = KV_GROUP * TM
    q = q_ref[...].reshape(rows, HEAD_DIM)
    m_ref[...] = jnp.full((rows, 1), -jnp.inf, jnp.float32)
    l_ref[...] = jnp.zeros((rows, 1), jnp.float32)
    acc_ref[...] = jnp.zeros((rows, HEAD_DIM), jnp.float32)

    def step(start, size):
        k = k_ref[0, pl.ds(start, size), :]
        v = v_ref[0, pl.ds(start, size), :]
        s = lax.dot_general(q, k, (((1,), (1,)), ((), ())), preferred_element_type=jnp.float32)
        m_old = m_ref[...]
        m_new = jnp.maximum(m_old, jnp.max(s, axis=1, keepdims=True))
        alpha = jnp.exp2(m_old - m_new)
        p = jnp.exp2(s - m_new)
        l_ref[...] = alpha * l_ref[...] + jnp.sum(p, axis=1, keepdims=True)
        acc_ref[...] = alpha * acc_ref[...] + jnp.dot(p.astype(jnp.bfloat16), v,
                                                      preferred_element_type=jnp.float32)
        m_ref[...] = m_new

    def body(j, carry):
        step(pl.multiple_of(j * TK, TK), TK)
        return carry

    lax.fori_loop(0, n_full, body, 0)
    if tail:
        step(n_full * TK, tail)
    o = acc_ref[...] / l_ref[...]
    for g in range(KV_GROUP):
        o_ref[:, g * HEAD_DIM:(g + 1) * HEAD_DIM] = o[g * TM:(g + 1) * TM].astype(o_ref.dtype)


def _attention(q, k, v, S):
    T = k.shape[1]
    n_full, tail = T // TK, T % TK
    rows = KV_GROUP * TM
    return pl.pallas_call(
        functools.partial(_attn_kernel, n_full, tail),
        grid=(N_KV_HEADS, S // TM),
        in_specs=[
            pl.BlockSpec((KV_GROUP, TM, HEAD_DIM), lambda g, i: (g, i, 0)),
            pl.BlockSpec((1, T, HEAD_DIM), lambda g, i: (g, 0, 0)),
            pl.BlockSpec((1, T, HEAD_DIM), lambda g, i: (g, 0, 0)),
        ],
        out_specs=pl.BlockSpec((TM, KV_GROUP * HEAD_DIM), lambda g, i: (i, g)),
        out_shape=jax.ShapeDtypeStruct((S, N_HEADS * HEAD_DIM), jnp.bfloat16),
        scratch_shapes=[
            pltpu.VMEM((rows, 1), jnp.float32),
            pltpu.VMEM((rows, 1), jnp.float32),
            pltpu.VMEM((rows, HEAD_DIM), jnp.float32),
        ],
        compiler_params=_cparams(("arbitrary", "arbitrary")),
        name="flash_attention",
    )(q, k, v)


def _router_out_shapes(S):
    return (
        jax.ShapeDtypeStruct((S, D_MODEL), jnp.float32),
        jax.ShapeDtypeStruct((S * SUBLANES, LANES), jnp.float32),
        jax.ShapeDtypeStruct((S, LANES), jnp.int32),
        jax.ShapeDtypeStruct((S, LANES), jnp.float32),
        jax.ShapeDtypeStruct((SUBLANES, LANES), jnp.int32),
    )


def _router_out_specs():
    return (
        pl.BlockSpec((TM, D_MODEL), lambda i: (i, 0)),
        pl.BlockSpec((TM * SUBLANES, LANES), lambda i: (i, 0)),
        pl.BlockSpec((TM, LANES), lambda i: (i, 0)),
        pl.BlockSpec((TM, LANES), lambda i: (i, 0)),
        pl.BlockSpec((SUBLANES, LANES), lambda i: (0, 0)),
    )


def _router_core(x_new, vec_ref, rw_ref, rb_ref, cnt_ref, x_out_ref, hf_ref, meta_ref, gate_ref,
                 counts_ref):
    i = pl.program_id(0)

    @pl.when(i == 0)
    def _():
        cnt_ref[...] = jnp.zeros_like(cnt_ref)

    x_out_ref[...] = x_new
    hf = _rms(x_new) * vec_ref[3:4, :] * (1.0 + vec_ref[2:3, :]) + vec_ref[1:2, :]
    for c in range(D_CHUNKS):
        hf_ref[pl.ds(c, TM, stride=SUBLANES), :] = hf[:, c * LANES:(c + 1) * LANES]

    logits = jnp.dot(hf, rw_ref[...], precision=_HI, preferred_element_type=jnp.float32) + rb_ref[...]
    lane = lax.broadcasted_iota(jnp.int32, logits.shape, 1)
    lane_f = lane.astype(jnp.float32)
    vals, idxs, hots = [], [], []
    work = logits
    for _ in range(TOP_K):
        mx = jnp.max(work, axis=1, keepdims=True)
        idx = jnp.min(jnp.where(work == mx, lane_f, float(LANES)), axis=1, keepdims=True)
        hot = lane_f == idx
        work = jnp.where(hot, -jnp.inf, work)
        vals.append(mx)
        idxs.append(idx.astype(jnp.int32))
        hots.append(hot)
    exps = [jnp.exp(vk - vals[0]) for vk in vals]
    denom = exps[0] + exps[1] + exps[2] + exps[3]

    any_hot = hots[0] | hots[1] | hots[2] | hots[3]
    hot_f = jnp.where(any_hot, 1.0, 0.0)
    r_i = lax.broadcasted_iota(jnp.int32, (TM, TM), 0)
    c_i = lax.broadcasted_iota(jnp.int32, (TM, TM), 1)
    lower = jnp.where(c_i < r_i, 1.0, 0.0).astype(jnp.bfloat16)
    before = jnp.dot(lower, hot_f.astype(jnp.bfloat16), preferred_element_type=jnp.float32) + cnt_ref[0:1, :]
    cnt_ref[0:1, :] = cnt_ref[0:1, :] + jnp.sum(hot_f, axis=0, keepdims=True)

    meta = jnp.zeros(logits.shape, jnp.int32)
    gates = jnp.zeros(logits.shape, jnp.float32)
    for k in range(TOP_K):
        rank = jnp.sum(jnp.where(hots[k], before, 0.0), axis=1, keepdims=True).astype(jnp.int32)
        meta = jnp.where(lane == k, idxs[k], meta)
        meta = jnp.where(lane == TOP_K + k, rank, meta)
        gates = jnp.where(lane == k, exps[k] / denom, gates)
    meta_ref[...] = meta
    gate_ref[...] = gates
    counts_ref[...] = jnp.broadcast_to(cnt_ref[0:1, :], counts_ref.shape).astype(jnp.int32)


def _post_attn_kernel(x_ref, o_ref, wo_ref, vec_ref, rw_ref, rb_ref,
                      x_out_ref, hf_ref, meta_ref, gate_ref, counts_ref, cnt_ref):
    y = jnp.dot(o_ref[...], wo_ref[...], preferred_element_type=jnp.float32)
    x_new = x_ref[...] + vec_ref[0:1, :] * y
    _router_core(x_new, vec_ref, rw_ref, rb_ref, cnt_ref, x_out_ref, hf_ref, meta_ref, gate_ref,
                 counts_ref)


def _post_attn(x, o, w_o, vecs, rw, rb):
    S = x.shape[0]
    return pl.pallas_call(
        _post_attn_kernel,
        grid=(S // TM,),
        in_specs=[
            pl.BlockSpec((TM, D_MODEL), lambda i: (i, 0)),
            pl.BlockSpec((TM, D_MODEL), lambda i: (i, 0)),
            pl.BlockSpec((D_MODEL, D_MODEL), lambda i: (0, 0)),
            pl.BlockSpec((SUBLANES, D_MODEL), lambda i: (0, 0)),
            pl.BlockSpec((D_MODEL, LANES), lambda i: (0, 0)),
            pl.BlockSpec((1, LANES), lambda i: (0, 0)),
        ],
        out_specs=_router_out_specs(),
        out_shape=_router_out_shapes(S),
        scratch_shapes=[pltpu.VMEM((SUBLANES, LANES), jnp.float32)],
        compiler_params=_cparams(("arbitrary",)),
        name="post_attn_router",
    )(x, o, w_o, vecs, rw, rb)


def _pool_kernel(S, x_ref, prev_ref, next_ref, mvec_ref, pw_ref, vec_ref, rw_ref, rb_ref,
                 x_out_ref, hf_ref, meta_ref, gate_ref, counts_ref, cnt_ref, buf_ref):
    i = pl.program_id(0)
    n = pl.num_programs(0)

    def hmod(xx):
        return _rms(xx) * mvec_ref[2:3, :] * (1.0 + mvec_ref[1:2, :]) + mvec_ref[0:1, :]

    x_cur = x_ref[...]
    h_cur = hmod(x_cur)
    buf_ref[0:POOL_HALO, :] = jnp.where(i > 0, hmod(prev_ref[...]), 0.0)
    buf_ref[POOL_HALO:POOL_HALO + TM, :] = h_cur
    buf_ref[POOL_HALO + TM:, :] = jnp.where(i < n - 1, hmod(next_ref[...]), 0.0)

    t = i * TM + lax.broadcasted_iota(jnp.int32, (TM, 1), 0)
    ys = []
    for g, w in enumerate(POOL_WINDOWS):
        c0 = g * POOL_GROUP
        acc = None
        for d in range(-(w // 2), w - w // 2):
            piece = buf_ref[POOL_HALO + d:POOL_HALO + d + TM, c0:c0 + POOL_GROUP]
            acc = piece if acc is None else acc + piece
        cnt = jnp.minimum(t - w // 2 + w, S) - jnp.maximum(t - w // 2, 0)
        pooled = acc / cnt.astype(jnp.float32) - h_cur[:, c0:c0 + POOL_GROUP]
        ys.append(jnp.dot(pooled.astype(jnp.bfloat16), pw_ref[g], preferred_element_type=jnp.float32))
    y = jnp.concatenate(ys, axis=1) * mvec_ref[3:4, :]
    x_new = x_cur + vec_ref[0:1, :] * y
    _router_core(x_new, vec_ref, rw_ref, rb_ref, cnt_ref, x_out_ref, hf_ref, meta_ref, gate_ref,
                 counts_ref)


def _pool(x, mvecs, pool_w, vecs, rw, rb):
    S = x.shape[0]
    hb = TM // POOL_HALO
    n_halo = S // POOL_HALO
    return pl.pallas_call(
        functools.partial(_pool_kernel, S),
        grid=(S // TM,),
        in_specs=[
            pl.BlockSpec((TM, D_MODEL), lambda i: (i, 0)),
            pl.BlockSpec((POOL_HALO, D_MODEL), lambda i: (jnp.maximum(i * hb - 1, 0), 0)),
            pl.BlockSpec((POOL_HALO, D_MODEL), lambda i: (jnp.minimum((i + 1) * hb, n_halo - 1), 0)),
            pl.BlockSpec((SUBLANES, D_MODEL), lambda i: (0, 0)),
            pl.BlockSpec((len(POOL_WINDOWS), POOL_GROUP, POOL_GROUP), lambda i: (0, 0, 0)),
            pl.BlockSpec((SUBLANES, D_MODEL), lambda i: (0, 0)),
            pl.BlockSpec((D_MODEL, LANES), lambda i: (0, 0)),
            pl.BlockSpec((1, LANES), lambda i: (0, 0)),
        ],
        out_specs=_router_out_specs(),
        out_shape=_router_out_shapes(S),
        scratch_shapes=[
            pltpu.VMEM((SUBLANES, LANES), jnp.float32),
            pltpu.VMEM((TM + 2 * POOL_HALO, D_MODEL), jnp.float32),
        ],
        compiler_params=_cparams(("arbitrary",)),
        name="pool_router",
    )(x, x, x, mvecs, pool_w, vecs, rw, rb)


def _dispatch_kernel(pos_ref, ztile_ref, hf_ref, zeros_ref, xs_ref, sem):
    i = pl.program_id(0)
    n = pl.num_programs(0)
    tile_rows = TM * SUBLANES
    pairs = TM * TOP_K

    def batch_wait(n_tokens):
        pltpu.make_async_copy(xs_ref.at[pl.ds(0, n_tokens * SUBLANES)],
                              xs_ref.at[pl.ds(0, n_tokens * SUBLANES)], sem).wait()

    @pl.when(i == 0)
    def _():
        def zero(e, carry):
            start = pl.multiple_of(ztile_ref[e] * tile_rows, tile_rows)
            pltpu.make_async_copy(zeros_ref, xs_ref.at[pl.ds(start, tile_rows)], sem).start()
            return carry

        lax.fori_loop(0, 2 * N_EXPERTS, zero, 0)
        batch_wait(N_EXPERTS * TM)
        batch_wait(N_EXPERTS * TM)

    def issue(r, carry):
        src = pl.multiple_of((i * TM + r) * SUBLANES, SUBLANES)
        for k in range(TOP_K):
            dst = pl.multiple_of(pos_ref[(i * TM + r) * TOP_K + k] * SUBLANES, SUBLANES)
            pltpu.make_async_copy(hf_ref.at[pl.ds(src, SUBLANES)],
                                  xs_ref.at[pl.ds(dst, SUBLANES)], sem).start()
        return carry

    lax.fori_loop(0, TM, issue, 0)

    @pl.when(i > 0)
    def _():
        batch_wait(pairs)

    @pl.when(i == n - 1)
    def _():
        batch_wait(pairs)


def _dispatch(pos, ztile, hf_tiles, n_rows):
    S = hf_tiles.shape[0] // SUBLANES
    zeros = jnp.zeros((TM * SUBLANES, LANES), jnp.float32)
    return pl.pallas_call(
        _dispatch_kernel,
        grid_spec=pltpu.PrefetchScalarGridSpec(
            num_scalar_prefetch=2,
            grid=(S // TM,),
            in_specs=[pl.BlockSpec(memory_space=pl.ANY), pl.BlockSpec(memory_space=pl.ANY)],
            out_specs=pl.BlockSpec(memory_space=pl.ANY),
            scratch_shapes=[pltpu.SemaphoreType.DMA(())],
        ),
        out_shape=jax.ShapeDtypeStruct((n_rows * SUBLANES, LANES), jnp.float32),
        compiler_params=_cparams(("arbitrary",)),
        name="moe_dispatch",
    )(pos, ztile, hf_tiles, zeros)


def _expert_kernel(te_ref, nused_ref, xs_ref, wgu_ref, bgu_ref, wd_ref, bd_ref, ys_ref):
    i = pl.program_id(0)

    @pl.when(i < nused_ref[0])
    def _():
        cols = [xs_ref[pl.ds(c, TM, stride=SUBLANES), :] for c in range(D_CHUNKS)]
        x = jnp.concatenate(cols, axis=1).astype(jnp.bfloat16)
        gu = jnp.dot(x, wgu_ref[0], preferred_element_type=jnp.float32) + bgu_ref[0]
        gate = jnp.minimum(gu[:, :D_FF], SWIGLU_LIMIT)
        up = jnp.clip(gu[:, D_FF:], -SWIGLU_LIMIT, SWIGLU_LIMIT)
        glu = gate / (1.0 + jnp.exp(-SWIGLU_ALPHA * gate))
        a = ((up + 1.0) * glu).astype(jnp.bfloat16)
        y = jnp.dot(a, wd_ref[0], preferred_element_type=jnp.float32) + bd_ref[0]
        for c in range(D_CHUNKS):
            ys_ref[pl.ds(c, TM, stride=SUBLANES), :] = y[:, c * LANES:(c + 1) * LANES]

    @pl.when(i >= nused_ref[0])
    def _():
        ys_ref[...] = jnp.zeros_like(ys_ref)


def _experts(tile_expert, n_used, xs, w_gu, b_gu, w_down, b_down):
    n_tiles = tile_expert.shape[0]
    E = w_gu.shape[0]

    def row_map(i, te, nu):
        return (jnp.minimum(i, nu[0] - 1), 0)

    def w_map(i, te, nu):
        return (te[i], 0, 0)

    return pl.pallas_call(
        _expert_kernel,
        grid_spec=pltpu.PrefetchScalarGridSpec(
            num_scalar_prefetch=2,
            grid=(n_tiles,),
            in_specs=[
                pl.BlockSpec((TM * SUBLANES, LANES), row_map),
                pl.BlockSpec((1, D_MODEL, 2 * D_FF), w_map),
                pl.BlockSpec((1, 1, 2 * D_FF), w_map),
                pl.BlockSpec((1, D_FF, D_MODEL), w_map),
                pl.BlockSpec((1, 1, D_MODEL), w_map),
            ],
            out_specs=pl.BlockSpec((TM * SUBLANES, LANES), lambda i, te, nu: (i, 0)),
        ),
        out_shape=jax.ShapeDtypeStruct(xs.shape, jnp.float32),
        compiler_params=_cparams(("arbitrary",)),
        name="moe_experts",
    )(tile_expert, n_used, xs, w_gu, b_gu.reshape(E, 1, 2 * D_FF), w_down, b_down.reshape(E, 1, D_MODEL))


def _combine_kernel(final, pos_ref, ys_ref, x_ref, gate_ref, vec_ref, o_ref, buf_ref, sem):
    i = pl.program_id(0)
    n = pl.num_programs(0)
    slot_rows = TOP_K * TM * SUBLANES

    def issue(tile, slot):
        def one(r, carry):
            for k in range(TOP_K):
                src = pl.multiple_of(pos_ref[(tile * TM + r) * TOP_K + k] * SUBLANES, SUBLANES)
                dst = pl.multiple_of(slot * slot_rows + (k * TM + r) * SUBLANES, SUBLANES)
                pltpu.make_async_copy(ys_ref.at[pl.ds(src, SUBLANES)],
                                      buf_ref.at[pl.ds(dst, SUBLANES)], sem.at[slot]).start()
            return carry

        lax.fori_loop(0, TM, one, 0)

    @pl.when(i == 0)
    def _():
        issue(0, 0)

    @pl.when(i + 1 < n)
    def _():
        issue(i + 1, (i + 1) % 2)

    slot = i % 2
    base = pl.multiple_of(slot * slot_rows, slot_rows)
    pltpu.make_async_copy(ys_ref.at[pl.ds(0, slot_rows)], buf_ref.at[pl.ds(base, slot_rows)],
                          sem.at[slot]).wait()

    gates = gate_ref[...]
    pieces = []
    for c in range(D_CHUNKS):
        f = None
        for k in range(TOP_K):
            rows = buf_ref[pl.ds(base + k * TM * SUBLANES + c, TM, stride=SUBLANES), :]
            term = gates[:, k:k + 1] * rows
            f = term if f is None else f + term
        pieces.append(f)
    f = jnp.concatenate(pieces, axis=1)
    x_new = x_ref[...] + vec_ref[0:1, :] * f
    if final:
        x_new = _rms(x_new) * vec_ref[1:2, :]
    o_ref[...] = x_new


def _combine(pos, ys, x, gates, vecs, final):
    S = x.shape[0]
    return pl.pallas_call(
        functools.partial(_combine_kernel, final),
        grid_spec=pltpu.PrefetchScalarGridSpec(
            num_scalar_prefetch=1,
            grid=(S // TM,),
            in_specs=[
                pl.BlockSpec(memory_space=pl.ANY),
                pl.BlockSpec((TM, D_MODEL), lambda i, p: (i, 0)),
                pl.BlockSpec((TM, LANES), lambda i, p: (i, 0)),
                pl.BlockSpec((SUBLANES, D_MODEL), lambda i, p: (0, 0)),
            ],
            out_specs=pl.BlockSpec((TM, D_MODEL), lambda i, p: (i, 0)),
            scratch_shapes=[
                pltpu.VMEM((2 * TOP_K * TM * SUBLANES, LANES), jnp.float32),
                pltpu.SemaphoreType.DMA((2,)),
            ],
        ),
        out_shape=jax.ShapeDtypeStruct((S, D_MODEL), jnp.float32),
        compiler_params=_cparams(("arbitrary",)),
        name="moe_combine",
    )(pos, ys, x, gates, vecs)


def _moe(x_new, hf_tiles, meta, gates, counts, w_gu, b_gu, w_down, b_down, out_vecs, final):
    S = x_new.shape[0]
    n_tiles = (S * TOP_K) // TM + N_EXPERTS
    cnt = counts[0, :N_EXPERTS]
    padded = ((cnt + TM - 1) // TM) * TM
    ends = jnp.cumsum(padded)
    offs = ends - padded
    eid = meta[:, :TOP_K]
    rank = meta[:, TOP_K:2 * TOP_K]
    pos = (offs[eid] + rank).reshape(-1).astype(jnp.int32)
    tile_ends = ends // TM
    tile_expert = jnp.minimum(
        jnp.searchsorted(tile_ends, jnp.arange(n_tiles, dtype=jnp.int32), side="right"),
        N_EXPERTS - 1).astype(jnp.int32)
    n_used = tile_ends[-1:].astype(jnp.int32)
    ztile = jnp.concatenate([
        jnp.maximum(tile_ends - 1, 0),
        jnp.minimum(tile_ends[-1] + jnp.arange(N_EXPERTS), n_tiles - 1)]).astype(jnp.int32)
    xs = _dispatch(pos, ztile, hf_tiles, n_tiles * TM)
    ys = _experts(tile_expert, n_used, xs, w_gu, b_gu, w_down, b_down)
    return _combine(pos, ys, x_new, gates, out_vecs, final)


def _pad_rows(rows):
    rows = [r.reshape(1, -1) for r in rows]
    rows += [jnp.zeros_like(rows[0])] * (SUBLANES - len(rows))
    return jnp.concatenate(rows, axis=0)


def _rope_tables(S, C):
    quarter = HEAD_DIM // 4
    inv_freq = ROPE_THETA ** (-np.arange(quarter, dtype=np.float32) / quarter)
    n_rows = S // GRID_W
    ang_r = np.arange(n_rows, dtype=np.float32)[:, None] * inv_freq[None, :]
    ang_c = np.arange(GRID_W, dtype=np.float32)[:, None] * inv_freq[None, :]

    def expand(tab_r, tab_c, sign):
        r = jnp.repeat(jnp.asarray(tab_r, jnp.float32), GRID_W, axis=0)
        cc = jnp.tile(jnp.asarray(tab_c, jnp.float32), (n_rows, 1))
        return jnp.concatenate([sign * r, r, sign * cc, cc], axis=1)

    cos = expand(np.cos(ang_r), np.cos(ang_c), 1.0)
    sin = expand(np.sin(ang_r), np.sin(ang_c), -1.0)
    cos = jnp.concatenate([cos, jnp.ones((C, HEAD_DIM), jnp.float32)], axis=0)
    sin = jnp.concatenate([sin, jnp.zeros((C, HEAD_DIM), jnp.float32)], axis=0)
    return cos, sin


def kernel(x, c, ctx, c_ctx, ada_w, ada_b, norm_mix, norm_ffn, attn_w_qkv, attn_q_norm, attn_k_norm,
           attn_w_o, pool_w, pool_scale, moe_router_w, moe_router_b, moe_w_gu, moe_b_gu, moe_w_down,
           moe_b_down, final_norm):
    B, S, D = x.shape
    C = ctx.shape[1]
    assert B == 1 and D == D_MODEL and S % TM == 0 and C % TM == 0 and S % GRID_W == 0
    x2d = x.reshape(S, D)
    ctx2d = ctx.reshape(C, D)
    bf = jnp.bfloat16

    cvec = _pad_rows([c.reshape(-1), c_ctx])
    mod = _modulation(cvec, ada_w, ada_b)
    m_l = [mod[l, 0].reshape(N_MOD, D) for l in range(2)]
    m_c0 = mod[0, 1].reshape(N_MOD, D)

    rw = [jnp.pad(moe_router_w[l], ((0, 0), (0, LANES - N_EXPERTS))) for l in range(2)]
    rb = [jnp.pad(moe_router_b[l], (0, LANES - N_EXPERTS), constant_values=NEG_BIG).reshape(1, LANES)
          for l in range(2)]

    cos_t, sin_t = _rope_tables(S, C)
    qkv_vecs = _pad_rows([m_l[0][0], m_l[0][1], m_c0[0], m_c0[1], norm_mix[0]])
    q, k, v = _qkv(x2d, ctx2d, qkv_vecs, attn_w_qkv[0].astype(bf), attn_q_norm[0].reshape(1, -1),
                   attn_k_norm[0].reshape(1, -1), cos_t, sin_t)
    o = _attention(q, k, v, S)
    vecs0 = _pad_rows([m_l[0][2], m_l[0][3], m_l[0][4], norm_ffn[0]])
    x1, hf, meta, gates, counts = _post_attn(x2d, o, attn_w_o[0].astype(bf), vecs0, rw[0], rb[0])
    x2 = _moe(x1, hf, meta, gates, counts, moe_w_gu[0].astype(bf), moe_b_gu[0], moe_w_down[0].astype(bf),
              moe_b_down[0], _pad_rows([m_l[0][5]]), final=False)

    mvecs = _pad_rows([m_l[1][0], m_l[1][1], norm_mix[1], pool_scale[0]])
    vecs1 = _pad_rows([m_l[1][2], m_l[1][3], m_l[1][4], norm_ffn[1]])
    x3, hf, meta, gates, counts = _pool(x2, mvecs, pool_w[0].astype(bf), vecs1, rw[1], rb[1])
    out = _moe(x3, hf, meta, gates, counts, moe_w_gu[1].astype(bf), moe_b_gu[1], moe_w_down[1].astype(bf),
               moe_b_down[1], _pad_rows([m_l[1][5], final_norm]), final=True)
    return out.reshape(B, S, D)
```

```python
import functools

import numpy as np
import jax
import jax.numpy as jnp
from jax import lax
from jax.experimental import pallas as pl
from jax.experimental.pallas import tpu as pltpu

D_MODEL = 1024
GRID_W = 64
N_HEADS = 8
N_KV_HEADS = 2
KV_GROUP = N_HEADS // N_KV_HEADS
HEAD_DIM = 128
ROPE_THETA = 10000.0
POOL_WINDOWS = (2, 4, 8, 16)
POOL_GROUP = D_MODEL // len(POOL_WINDOWS)
POOL_HALO = 8
N_EXPERTS = 32
TOP_K = 4
D_FF = D_MODEL
SWIGLU_LIMIT = 7.0
SWIGLU_ALPHA = 1.702
NORM_EPS = 1e-6
N_MOD = 6

LANES = 128
SUBLANES = 8
D_CHUNKS = D_MODEL // LANES
TM = 256
TK_MAX = 1280
VMEM_LIMIT = 48 * 1024 * 1024
ATTN_VMEM_LIMIT = 56 * 1024 * 1024
LOG2E = 1.4426950408889634
NEG_BIG = -1e30

_HI = lax.Precision.HIGHEST


def _cparams(sem):
    return pltpu.CompilerParams(dimension_semantics=sem, vmem_limit_bytes=VMEM_LIMIT)


def _rms(x, eps=NORM_EPS):
    return x * lax.rsqrt(jnp.mean(x * x, axis=-1, keepdims=True) + eps)


def _mod_kernel(cv_ref, w_ref, b_ref, o_ref):
    cv = cv_ref[...]
    s = cv / (1.0 + jnp.exp(-cv))
    o_ref[0] = jnp.dot(s, w_ref[0], precision=_HI, preferred_element_type=jnp.float32) + b_ref[0]


def _modulation(cvec, ada_w, ada_b):
    depth = ada_w.shape[0]
    nblk = 4
    bw = N_MOD * D_MODEL // nblk
    return pl.pallas_call(
        _mod_kernel,
        grid=(depth, nblk),
        in_specs=[
            pl.BlockSpec((SUBLANES, D_MODEL), lambda l, j: (0, 0)),
            pl.BlockSpec((1, D_MODEL, bw), lambda l, j: (l, 0, j)),
            pl.BlockSpec((1, 1, bw), lambda l, j: (l, 0, j)),
        ],
        out_specs=pl.BlockSpec((1, SUBLANES, bw), lambda l, j: (l, 0, j)),
        out_shape=jax.ShapeDtypeStruct((depth, SUBLANES, N_MOD * D_MODEL), jnp.float32),
        compiler_params=_cparams(("arbitrary", "arbitrary")),
        name="modulation",
    )(cvec, ada_w, ada_b.reshape(depth, 1, N_MOD * D_MODEL))


def _rope(x, cos, sin_signed):
    lane = lax.broadcasted_iota(jnp.int32, x.shape, 1)
    partner = jnp.where((lane % 64) < 32, pltpu.roll(x, LANES - 32, 1), pltpu.roll(x, 32, 1))
    return x * cos + partner * sin_signed


def _qkv_kernel(n_lat_tiles, x_ref, ctx_ref, vec_ref, w_ref, qg_ref, kg_ref, cos_ref, sin_ref,
                q_ref, k_ref, v_ref):
    i = pl.program_id(0)
    is_ctx = i >= n_lat_tiles
    xt = jnp.where(is_ctx, ctx_ref[...], x_ref[...])
    sh = jnp.where(is_ctx, vec_ref[2:3, :], vec_ref[0:1, :])
    sc = jnp.where(is_ctx, vec_ref[3:4, :], vec_ref[1:2, :])
    h = _rms(xt) * vec_ref[4:5, :] * (1.0 + sc) + sh
    qkv = jnp.dot(h.astype(jnp.bfloat16), w_ref[...], preferred_element_type=jnp.float32)
    cos = cos_ref[...]
    sin = sin_ref[...]
    q_scale = (HEAD_DIM ** -0.5) * LOG2E
    for hd in range(N_HEADS):
        qh = _rms(qkv[:, hd * HEAD_DIM:(hd + 1) * HEAD_DIM]) * qg_ref[...]
        q_ref[hd] = (_rope(qh, cos, sin) * q_scale).astype(jnp.bfloat16)
    for kv in range(N_KV_HEADS):
        c0 = (N_HEADS + kv) * HEAD_DIM
        kh = _rms(qkv[:, c0:c0 + HEAD_DIM]) * kg_ref[...]
        k_ref[kv] = _rope(kh, cos, sin).astype(jnp.bfloat16)
        c1 = (N_HEADS + N_KV_HEADS + kv) * HEAD_DIM
        v_ref[kv] = qkv[:, c1:c1 + HEAD_DIM].astype(jnp.bfloat16)


def _qkv(x, ctx, vecs, w_qkv, q_g, k_g, cos_t, sin_t):
    S, C = x.shape[0], ctx.shape[0]
    n_lat = S // TM
    n_ctx = C // TM
    T = S + C
    last = n_lat - 1
    qkv_w = w_qkv.shape[1]
    out_shapes = (
        jax.ShapeDtypeStruct((N_HEADS, T, HEAD_DIM), jnp.bfloat16),
        jax.ShapeDtypeStruct((N_KV_HEADS, T, HEAD_DIM), jnp.bfloat16),
        jax.ShapeDtypeStruct((N_KV_HEADS, T, HEAD_DIM), jnp.bfloat16),
    )
    return pl.pallas_call(
        functools.partial(_qkv_kernel, n_lat),
        grid=(n_lat + n_ctx,),
        in_specs=[
            pl.BlockSpec((TM, D_MODEL), lambda i: (jnp.minimum(i, last), 0)),
            pl.BlockSpec((TM, D_MODEL), lambda i: (jnp.maximum(i - n_lat, 0), 0)),
            pl.BlockSpec((SUBLANES, D_MODEL), lambda i: (0, 0)),
            pl.BlockSpec((D_MODEL, qkv_w), lambda i: (0, 0)),
            pl.BlockSpec((1, HEAD_DIM), lambda i: (0, 0)),
            pl.BlockSpec((1, HEAD_DIM), lambda i: (0, 0)),
            pl.BlockSpec((TM, HEAD_DIM), lambda i: (i, 0)),
            pl.BlockSpec((TM, HEAD_DIM), lambda i: (i, 0)),
        ],
        out_specs=(
            pl.BlockSpec((N_HEADS, TM, HEAD_DIM), lambda i: (0, i, 0)),
            pl.BlockSpec((N_KV_HEADS, TM, HEAD_DIM), lambda i: (0, i, 0)),
            pl.BlockSpec((N_KV_HEADS, TM, HEAD_DIM), lambda i: (0, i, 0)),
        ),
        out_shape=out_shapes,
        compiler_params=_cparams(("arbitrary",)),
        name="qkv_proj",
    )(x, ctx, vecs, w_qkv, q_g, k_g, cos_t, sin_t)


def _attn_kernel(n_chunks, tk, q_ref, k_ref, v_ref, o_ref, s_ref, p_ref, a_ref, mc_ref, m_ref, acc_ref):
    rows = KV_GROUP * TM
    ones = jnp.ones((tk, HEAD_DIM), jnp.bfloat16)

    def scores(c, slot):
        k = k_ref[0, pl.ds(pl.multiple_of(c * tk, tk), tk), :]
        q = q_ref[...].reshape(rows, HEAD_DIM)
        s = lax.dot_general(q, k, (((1,), (1,)), ((), ())), preferred_element_type=jnp.float32)
        s_ref[slot] = s
        mc_ref[slot] = jnp.max(s, axis=1, keepdims=True)

    def softmax(slot):
        m_old = m_ref[...]
        m_new = jnp.maximum(m_old, mc_ref[slot])
        a_ref[slot] = jnp.exp2(m_old - m_new)
        m_ref[...] = m_new
        p_ref[slot] = jnp.exp2(s_ref[slot] - m_new).astype(jnp.bfloat16)

    def values(c, slot):
        v = v_ref[0, pl.ds(pl.multiple_of(c * tk, tk), tk), :]
        vx = jnp.concatenate([v, ones], axis=1)
        acc_ref[...] = acc_ref[...] * a_ref[slot] + jnp.dot(p_ref[slot], vx,
                                                           preferred_element_type=jnp.float32)

    def iteration(t, par):
        scores(t + 1, 1 - par)
        softmax(par)
        values(jnp.maximum(t - 1, 0), 1 - par)

    m_ref[...] = jnp.full(m_ref.shape, -jnp.inf, jnp.float32)
    acc_ref[...] = jnp.zeros(acc_ref.shape, jnp.float32)
    p_ref[1] = jnp.zeros(p_ref.shape[1:], jnp.bfloat16)
    a_ref[1] = jnp.ones(a_ref.shape[1:], jnp.float32)
    scores(0, 0)

    n_steady = n_chunks - 1

    def pair(u, carry):
        iteration(2 * u, 0)
        iteration(2 * u + 1, 1)
        return carry

    lax.fori_loop(0, n_steady // 2, pair, 0)
    if n_steady % 2:
        iteration(n_steady - 1, (n_steady - 1) % 2)
    last = n_chunks - 1
    softmax(last % 2)
    values(max(last - 1, 0), 1 - last % 2)
    values(last, last % 2)

    acc = acc_ref[...]
    o = acc[:, :HEAD_DIM] / acc[:, HEAD_DIM:]
    for g in range(KV_GROUP):
        o_ref[:, g * HEAD_DIM:(g + 1) * HEAD_DIM] = o[g * TM:(g + 1) * TM].astype(o_ref.dtype)


def _key_chunk(T):
    return max(c for c in range(LANES, TK_MAX + 1, LANES) if T % c == 0)


def _attention(q, k, v, S):
    T = k.shape[1]
    tk = _key_chunk(T)
    rows = KV_GROUP * TM
    return pl.pallas_call(
        functools.partial(_attn_kernel, T // tk, tk),
        grid=(N_KV_HEADS, S // TM),
        in_specs=[
            pl.BlockSpec((KV_GROUP, TM, HEAD_DIM), lambda g, i: (g, i, 0)),
            pl.BlockSpec((1, T, HEAD_DIM), lambda g, i: (g, 0, 0)),
            pl.BlockSpec((1, T, HEAD_DIM), lambda g, i: (g, 0, 0)),
        ],
        out_specs=pl.BlockSpec((TM, KV_GROUP * HEAD_DIM), lambda g, i: (i, g)),
        out_shape=jax.ShapeDtypeStruct((S, N_HEADS * HEAD_DIM), jnp.bfloat16),
        scratch_shapes=[
            pltpu.VMEM((2, rows, tk), jnp.float32),
            pltpu.VMEM((2, rows, tk), jnp.bfloat16),
            pltpu.VMEM((2, rows, 1), jnp.float32),
            pltpu.VMEM((2, rows, 1), jnp.float32),
            pltpu.VMEM((rows, 1), jnp.float32),
            pltpu.VMEM((rows, 2 * HEAD_DIM), jnp.float32),
        ],
        compiler_params=pltpu.CompilerParams(dimension_semantics=("arbitrary", "arbitrary"),
                                             vmem_limit_bytes=ATTN_VMEM_LIMIT),
        name="flash_attention",
    )(q, k, v)


def _router_out_shapes(S):
    return (
        jax.ShapeDtypeStruct((S, D_MODEL), jnp.float32),
        jax.ShapeDtypeStruct((S * SUBLANES, LANES), jnp.float32),
        jax.ShapeDtypeStruct((S, LANES), jnp.int32),
        jax.ShapeDtypeStruct((S, LANES), jnp.float32),
        jax.ShapeDtypeStruct((SUBLANES, LANES), jnp.int32),
    )


def _router_out_specs():
    return (
        pl.BlockSpec((TM, D_MODEL), lambda i: (i, 0)),
        pl.BlockSpec((TM * SUBLANES, LANES), lambda i: (i, 0)),
        pl.BlockSpec((TM, LANES), lambda i: (i, 0)),
        pl.BlockSpec((TM, LANES), lambda i: (i, 0)),
        pl.BlockSpec((SUBLANES, LANES), lambda i: (0, 0)),
    )


def _router_core(x_new, vec_ref, rw_ref, rb_ref, cnt_ref, x_out_ref, hf_ref, meta_ref, gate_ref,
                 counts_ref):
    i = pl.program_id(0)

    @pl.when(i == 0)
    def _():
        cnt_ref[...] = jnp.zeros_like(cnt_ref)

    x_out_ref[...] = x_new
    hf = _rms(x_new) * vec_ref[3:4, :] * (1.0 + vec_ref[2:3, :]) + vec_ref[1:2, :]
    for c in range(D_CHUNKS):
        hf_ref[pl.ds(c, TM, stride=SUBLANES), :] = hf[:, c * LANES:(c + 1) * LANES]

    logits = jnp.dot(hf, rw_ref[...], precision=_HI, preferred_element_type=jnp.float32) + rb_ref[...]
    lane = lax.broadcasted_iota(jnp.int32, logits.shape, 1)
    lane_f = lane.astype(jnp.float32)
    vals, idxs, hots = [], [], []
    work = logits
    for _ in range(TOP_K):
        mx = jnp.max(work, axis=1, keepdims=True)
        idx = jnp.min(jnp.where(work == mx, lane_f, float(LANES)), axis=1, keepdims=True)
        hot = lane_f == idx
        work = jnp.where(hot, -jnp.inf, work)
        vals.append(mx)
        idxs.append(idx.astype(jnp.int32))
        hots.append(hot)
    exps = [jnp.exp(vk - vals[0]) for vk in vals]
    denom = exps[0] + exps[1] + exps[2] + exps[3]

    any_hot = hots[0] | hots[1] | hots[2] | hots[3]
    hot_f = jnp.where(any_hot, 1.0, 0.0)
    r_i = lax.broadcasted_iota(jnp.int32, (TM, TM), 0)
    c_i = lax.broadcasted_iota(jnp.int32, (TM, TM), 1)
    lower = jnp.where(c_i < r_i, 1.0, 0.0).astype(jnp.bfloat16)
    before = jnp.dot(lower, hot_f.astype(jnp.bfloat16), preferred_element_type=jnp.float32) + cnt_ref[0:1, :]
    cnt_ref[0:1, :] = cnt_ref[0:1, :] + jnp.sum(hot_f, axis=0, keepdims=True)

    meta = jnp.zeros(logits.shape, jnp.int32)
    gates = jnp.zeros(logits.shape, jnp.float32)
    for k in range(TOP_K):
        rank = jnp.sum(jnp.where(hots[k], before, 0.0), axis=1, keepdims=True).astype(jnp.int32)
        meta = jnp.where(lane == k, idxs[k], meta)
        meta = jnp.where(lane == TOP_K + k, rank, meta)
        gates = jnp.where(lane == k, exps[k] / denom, gates)
    meta_ref[...] = meta
    gate_ref[...] = gates
    counts_ref[...] = jnp.broadcast_to(cnt_ref[0:1, :], counts_ref.shape).astype(jnp.int32)


def _post_attn_kernel(x_ref, o_ref, wo_ref, vec_ref, rw_ref, rb_ref,
                      x_out_ref, hf_ref, meta_ref, gate_ref, counts_ref, cnt_ref):
    y = jnp.dot(o_ref[...], wo_ref[...], preferred_element_type=jnp.float32)
    x_new = x_ref[...] + vec_ref[0:1, :] * y
    _router_core(x_new, vec_ref, rw_ref, rb_ref, cnt_ref, x_out_ref, hf_ref, meta_ref, gate_ref,
                 counts_ref)


def _post_attn(x, o, w_o, vecs, rw, rb):
    S = x.shape[0]
    return pl.pallas_call(
        _post_attn_kernel,
        grid=(S // TM,),
        in_specs=[
            pl.BlockSpec((TM, D_MODEL), lambda i: (i, 0)),
            pl.BlockSpec((TM, D_MODEL), lambda i: (i, 0)),
            pl.BlockSpec((D_MODEL, D_MODEL), lambda i: (0, 0)),
            pl.BlockSpec((SUBLANES, D_MODEL), lambda i: (0, 0)),
            pl.BlockSpec((D_MODEL, LANES), lambda i: (0, 0)),
            pl.BlockSpec((1, LANES), lambda i: (0, 0)),
        ],
        out_specs=_router_out_specs(),
        out_shape=_router_out_shapes(S),
        scratch_shapes=[pltpu.VMEM((SUBLANES, LANES), jnp.float32)],
        compiler_params=_cparams(("arbitrary",)),
        name="post_attn_router",
    )(x, o, w_o, vecs, rw, rb)


def _pool_kernel(S, x_ref, prev_ref, next_ref, mvec_ref, pw_ref, vec_ref, rw_ref, rb_ref,
                 x_out_ref, hf_ref, meta_ref, gate_ref, counts_ref, cnt_ref, buf_ref):
    i = pl.program_id(0)
    n = pl.num_programs(0)

    def hmod(xx):
        return _rms(xx) * mvec_ref[2:3, :] * (1.0 + mvec_ref[1:2, :]) + mvec_ref[0:1, :]

    x_cur = x_ref[...]
    h_cur = hmod(x_cur)
    buf_ref[0:POOL_HALO, :] = jnp.where(i > 0, hmod(prev_ref[...]), 0.0)
    buf_ref[POOL_HALO:POOL_HALO + TM, :] = h_cur
    buf_ref[POOL_HALO + TM:, :] = jnp.where(i < n - 1, hmod(next_ref[...]), 0.0)

    t = i * TM + lax.broadcasted_iota(jnp.int32, (TM, 1), 0)
    ys = []
    for g, w in enumerate(POOL_WINDOWS):
        c0 = g * POOL_GROUP
        acc = None
        for d in range(-(w // 2), w - w // 2):
            piece = buf_ref[POOL_HALO + d:POOL_HALO + d + TM, c0:c0 + POOL_GROUP]
            acc = piece if acc is None else acc + piece
        cnt = jnp.minimum(t - w // 2 + w, S) - jnp.maximum(t - w // 2, 0)
        pooled = acc / cnt.astype(jnp.float32) - h_cur[:, c0:c0 + POOL_GROUP]
        ys.append(jnp.dot(pooled.astype(jnp.bfloat16), pw_ref[g], preferred_element_type=jnp.float32))
    y = jnp.concatenate(ys, axis=1) * mvec_ref[3:4, :]
    x_new = x_cur + vec_ref[0:1, :] * y
    _router_core(x_new, vec_ref, rw_ref, rb_ref, cnt_ref, x_out_ref, hf_ref, meta_ref, gate_ref,
                 counts_ref)


def _pool(x, mvecs, pool_w, vecs, rw, rb):
    S = x.shape[0]
    hb = TM // POOL_HALO
    n_halo = S // POOL_HALO
    return pl.pallas_call(
        functools.partial(_pool_kernel, S),
        grid=(S // TM,),
        in_specs=[
            pl.BlockSpec((TM, D_MODEL), lambda i: (i, 0)),
            pl.BlockSpec((POOL_HALO, D_MODEL), lambda i: (jnp.maximum(i * hb - 1, 0), 0)),
            pl.BlockSpec((POOL_HALO, D_MODEL), lambda i: (jnp.minimum((i + 1) * hb, n_halo - 1), 0)),
            pl.BlockSpec((SUBLANES, D_MODEL), lambda i: (0, 0)),
            pl.BlockSpec((len(POOL_WINDOWS), POOL_GROUP, POOL_GROUP), lambda i: (0, 0, 0)),
            pl.BlockSpec((SUBLANES, D_MODEL), lambda i: (0, 0)),
            pl.BlockSpec((D_MODEL, LANES), lambda i: (0, 0)),
            pl.BlockSpec((1, LANES), lambda i: (0, 0)),
        ],
        out_specs=_router_out_specs(),
        out_shape=_router_out_shapes(S),
        scratch_shapes=[
            pltpu.VMEM((SUBLANES, LANES), jnp.float32),
            pltpu.VMEM((TM + 2 * POOL_HALO, D_MODEL), jnp.float32),
        ],
        compiler_params=_cparams(("arbitrary",)),
        name="pool_router",
    )(x, x, x, mvecs, pool_w, vecs, rw, rb)


def _dispatch_kernel(pos_ref, ztile_ref, hf_ref, xs_ref, stage_ref, zeros_ref, load_sem, row_sem, zero_sem):
    i = pl.program_id(0)
    n = pl.num_programs(0)
    tile_rows = TM * SUBLANES
    pairs = TM * TOP_K

    def load(tile, slot):
        src = pl.multiple_of(tile * tile_rows, tile_rows)
        dst = pl.multiple_of(slot * tile_rows, tile_rows)
        return pltpu.make_async_copy(hf_ref.at[pl.ds(src, tile_rows)],
                                     stage_ref.at[pl.ds(dst, tile_rows)], load_sem.at[slot])

    def rows_wait():
        pltpu.make_async_copy(xs_ref.at[pl.ds(0, pairs * SUBLANES)],
                              xs_ref.at[pl.ds(0, pairs * SUBLANES)], row_sem).wait()

    def zero_copy(e):
        start = pl.multiple_of(jnp.maximum(ztile_ref[e], 0) * tile_rows, tile_rows)
        return pltpu.make_async_copy(zeros_ref, xs_ref.at[pl.ds(start, tile_rows)], zero_sem)

    @pl.when(i == 0)
    def _():
        zeros_ref[...] = jnp.zeros_like(zeros_ref)
        load(0, 0).start()

        def zero_start(e, carry):
            @pl.when(ztile_ref[e] >= 0)
            def _():
                zero_copy(e).start()
            return carry

        def zero_wait(e, carry):
            @pl.when(ztile_ref[e] >= 0)
            def _():
                zero_copy(e).wait()
            return carry

        lax.fori_loop(0, 2 * N_EXPERTS, zero_start, 0)
        lax.fori_loop(0, 2 * N_EXPERTS, zero_wait, 0)

    @pl.when(i > 0)
    def _():
        rows_wait()

    @pl.when(i + 1 < n)
    def _():
        load(i + 1, (i + 1) % 2).start()

    slot = i % 2
    load(i, slot).wait()

    def issue(r, carry):
        src = pl.multiple_of(slot * tile_rows + r * SUBLANES, SUBLANES)
        for k in range(TOP_K):
            dst = pl.multiple_of(pos_ref[(i * TM + r) * TOP_K + k] * SUBLANES, SUBLANES)
            pltpu.make_async_copy(stage_ref.at[pl.ds(src, SUBLANES)],
                                  xs_ref.at[pl.ds(dst, SUBLANES)], row_sem).start()
        return carry

    lax.fori_loop(0, TM, issue, 0)

    @pl.when(i == n - 1)
    def _():
        rows_wait()


def _dispatch(pos, ztile, hf_tiles, n_rows):
    S = hf_tiles.shape[0] // SUBLANES
    tile_rows = TM * SUBLANES
    return pl.pallas_call(
        _dispatch_kernel,
        grid_spec=pltpu.PrefetchScalarGridSpec(
            num_scalar_prefetch=2,
            grid=(S // TM,),
            in_specs=[pl.BlockSpec(memory_space=pl.ANY)],
            out_specs=pl.BlockSpec(memory_space=pl.ANY),
            scratch_shapes=[
                pltpu.VMEM((2 * tile_rows, LANES), jnp.float32),
                pltpu.VMEM((tile_rows, LANES), jnp.float32),
                pltpu.SemaphoreType.DMA((2,)),
                pltpu.SemaphoreType.DMA(()),
                pltpu.SemaphoreType.DMA(()),
            ],
        ),
        out_shape=jax.ShapeDtypeStruct((n_rows * SUBLANES, LANES), jnp.float32),
        compiler_params=_cparams(("arbitrary",)),
        name="moe_dispatch",
    )(pos, ztile, hf_tiles)


def _expert_kernel(te_ref, nused_ref, xs_ref, wgu_ref, bgu_ref, wd_ref, bd_ref, ys_ref):
    i = pl.program_id(0)

    @pl.when(i < nused_ref[0])
    def _():
        cols = [xs_ref[pl.ds(c, TM, stride=SUBLANES), :] for c in range(D_CHUNKS)]
        x = jnp.concatenate(cols, axis=1).astype(jnp.bfloat16)
        gu = jnp.dot(x, wgu_ref[0], preferred_element_type=jnp.float32) + bgu_ref[0]
        gate = jnp.minimum(gu[:, :D_FF], SWIGLU_LIMIT)
        up = jnp.clip(gu[:, D_FF:], -SWIGLU_LIMIT, SWIGLU_LIMIT)
        glu = gate / (1.0 + jnp.exp(-SWIGLU_ALPHA * gate))
        a = ((up + 1.0) * glu).astype(jnp.bfloat16)
        y = jnp.dot(a, wd_ref[0], preferred_element_type=jnp.float32) + bd_ref[0]
        for c in range(D_CHUNKS):
            ys_ref[pl.ds(c, TM, stride=SUBLANES), :] = y[:, c * LANES:(c + 1) * LANES]

    @pl.when(i >= nused_ref[0])
    def _():
        ys_ref[...] = jnp.zeros_like(ys_ref)


def _experts(tile_expert, n_used, xs, w_gu, b_gu, w_down, b_down):
    n_tiles = tile_expert.shape[0]
    E = w_gu.shape[0]

    def row_map(i, te, nu):
        return (jnp.minimum(i, nu[0] - 1), 0)

    def w_map(i, te, nu):
        return (te[i], 0, 0)

    return pl.pallas_call(
        _expert_kernel,
        grid_spec=pltpu.PrefetchScalarGridSpec(
            num_scalar_prefetch=2,
            grid=(n_tiles,),
            in_specs=[
                pl.BlockSpec((TM * SUBLANES, LANES), row_map),
                pl.BlockSpec((1, D_MODEL, 2 * D_FF), w_map),
                pl.BlockSpec((1, 1, 2 * D_FF), w_map),
                pl.BlockSpec((1, D_FF, D_MODEL), w_map),
                pl.BlockSpec((1, 1, D_MODEL), w_map),
            ],
            out_specs=pl.BlockSpec((TM * SUBLANES, LANES), lambda i, te, nu: (i, 0)),
        ),
        out_shape=jax.ShapeDtypeStruct(xs.shape, jnp.float32),
        compiler_params=_cparams(("arbitrary",)),
        name="moe_experts",
    )(tile_expert, n_used, xs, w_gu, b_gu.reshape(E, 1, 2 * D_FF), w_down, b_down.reshape(E, 1, D_MODEL))


def _combine_kernel(final, pos_ref, ys_ref, x_ref, gate_ref, vec_ref, o_ref, buf_ref, sem):
    i = pl.program_id(0)
    n = pl.num_programs(0)
    slot_rows = TOP_K * TM * SUBLANES

    def issue(tile, slot):
        def one(r, carry):
            for k in range(TOP_K):
                src = pl.multiple_of(pos_ref[(tile * TM + r) * TOP_K + k] * SUBLANES, SUBLANES)
                dst = pl.multiple_of(slot * slot_rows + (k * TM + r) * SUBLANES, SUBLANES)
                pltpu.make_async_copy(ys_ref.at[pl.ds(src, SUBLANES)],
                                      buf_ref.at[pl.ds(dst, SUBLANES)], sem.at[slot]).start()
            return carry

        lax.fori_loop(0, TM, one, 0)

    @pl.when(i == 0)
    def _():
        issue(0, 0)

    @pl.when(i + 1 < n)
    def _():
        issue(i + 1, (i + 1) % 2)

    slot = i % 2
    base = pl.multiple_of(slot * slot_rows, slot_rows)
    pltpu.make_async_copy(ys_ref.at[pl.ds(0, slot_rows)], buf_ref.at[pl.ds(base, slot_rows)],
                          sem.at[slot]).wait()

    gates = gate_ref[...]
    pieces = []
    for c in range(D_CHUNKS):
        f = None
        for k in range(TOP_K):
            rows = buf_ref[pl.ds(base + k * TM * SUBLANES + c, TM, stride=SUBLANES), :]
            term = gates[:, k:k + 1] * rows
            f = term if f is None else f + term
        pieces.append(f)
    f = jnp.concatenate(pieces, axis=1)
    x_new = x_ref[...] + vec_ref[0:1, :] * f
    if final:
        x_new = _rms(x_new) * vec_ref[1:2, :]
    o_ref[...] = x_new


def _combine(pos, ys, x, gates, vecs, final):
    S = x.shape[0]
    return pl.pallas_call(
        functools.partial(_combine_kernel, final),
        grid_spec=pltpu.PrefetchScalarGridSpec(
            num_scalar_prefetch=1,
            grid=(S // TM,),
            in_specs=[
                pl.BlockSpec(memory_space=pl.ANY),
                pl.BlockSpec((TM, D_MODEL), lambda i, p: (i, 0)),
                pl.BlockSpec((TM, LANES), lambda i, p: (i, 0)),
                pl.BlockSpec((SUBLANES, D_MODEL), lambda i, p: (0, 0)),
            ],
            out_specs=pl.BlockSpec((TM, D_MODEL), lambda i, p: (i, 0)),
            scratch_shapes=[
                pltpu.VMEM((2 * TOP_K * TM * SUBLANES, LANES), jnp.float32),
                pltpu.SemaphoreType.DMA((2,)),
            ],
        ),
        out_shape=jax.ShapeDtypeStruct((S, D_MODEL), jnp.float32),
        compiler_params=_cparams(("arbitrary",)),
        name="moe_combine",
    )(pos, ys, x, gates, vecs)


def _moe(x_new, hf_tiles, meta, gates, counts, w_gu, b_gu, w_down, b_down, out_vecs, final):
    S = x_new.shape[0]
    n_tiles = (S * TOP_K) // TM + N_EXPERTS
    cnt = counts[0, :N_EXPERTS]
    padded = ((cnt + TM - 1) // TM) * TM
    ends = jnp.cumsum(padded)
    offs = ends - padded
    eid = meta[:, :TOP_K]
    rank = meta[:, TOP_K:2 * TOP_K]
    pos = (offs[eid] + rank).reshape(-1).astype(jnp.int32)
    tile_ends = ends // TM
    tile_ids = jnp.arange(n_tiles, dtype=jnp.int32)
    tile_expert = jnp.minimum(
        jnp.sum((tile_ends[None, :] <= tile_ids[:, None]).astype(jnp.int32), axis=1),
        N_EXPERTS - 1).astype(jnp.int32)
    n_used = tile_ends[-1:].astype(jnp.int32)
    trailing = tile_ends[-1] + jnp.arange(N_EXPERTS, dtype=jnp.int32)
    ztile = jnp.concatenate([
        jnp.where(padded > 0, tile_ends - 1, -1),
        jnp.where(trailing < n_tiles, trailing, -1)]).astype(jnp.int32)
    xs = _dispatch(pos, ztile, hf_tiles, n_tiles * TM)
    ys = _experts(tile_expert, n_used, xs, w_gu, b_gu, w_down, b_down)
    return _combine(pos, ys, x_new, gates, out_vecs, final)


def _pad_rows(rows):
    rows = [r.reshape(1, -1) for r in rows]
    rows += [jnp.zeros_like(rows[0])] * (SUBLANES - len(rows))
    return jnp.concatenate(rows, axis=0)


def _rope_tables(S, C):
    quarter = HEAD_DIM // 4
    inv_freq = ROPE_THETA ** (-np.arange(quarter, dtype=np.float32) / quarter)
    n_rows = S // GRID_W
    ang_r = np.arange(n_rows, dtype=np.float32)[:, None] * inv_freq[None, :]
    ang_c = np.arange(GRID_W, dtype=np.float32)[:, None] * inv_freq[None, :]

    def expand(tab_r, tab_c, sign):
        r = jnp.repeat(jnp.asarray(tab_r, jnp.float32), GRID_W, axis=0)
        cc = jnp.tile(jnp.asarray(tab_c, jnp.float32), (n_rows, 1))
        return jnp.concatenate([sign * r, r, sign * cc, cc], axis=1)

    cos = expand(np.cos(ang_r), np.cos(ang_c), 1.0)
    sin = expand(np.sin(ang_r), np.sin(ang_c), -1.0)
    cos = jnp.concatenate([cos, jnp.ones((C, HEAD_DIM), jnp.float32)], axis=0)
    sin = jnp.concatenate([sin, jnp.zeros((C, HEAD_DIM), jnp.float32)], axis=0)
    return cos, sin


def kernel(x, c, ctx, c_ctx, ada_w, ada_b, norm_mix, norm_ffn, attn_w_qkv, attn_q_norm, attn_k_norm,
           attn_w_o, pool_w, pool_scale, moe_router_w, moe_router_b, moe_w_gu, moe_b_gu, moe_w_down,
           moe_b_down, final_norm):
    B, S, D = x.shape
    C = ctx.shape[1]
    assert B == 1 and D == D_MODEL and S % TM == 0 and C % TM == 0 and S % GRID_W == 0
    x2d = x.reshape(S, D)
    ctx2d = ctx.reshape(C, D)
    bf = jnp.bfloat16

    cvec = _pad_rows([c.reshape(-1), c_ctx])
    mod = _modulation(cvec, ada_w, ada_b)
    m_l = [mod[l, 0].reshape(N_MOD, D) for l in range(2)]
    m_c0 = mod[0, 1].reshape(N_MOD, D)

    rw = [jnp.pad(moe_router_w[l], ((0, 0), (0, LANES - N_EXPERTS))) for l in range(2)]
    rb = [jnp.pad(moe_router_b[l], (0, LANES - N_EXPERTS), constant_values=NEG_BIG).reshape(1, LANES)
          for l in range(2)]

    cos_t, sin_t = _rope_tables(S, C)
    qkv_vecs = _pad_rows([m_l[0][0], m_l[0][1], m_c0[0], m_c0[1], norm_mix[0]])
    q, k, v = _qkv(x2d, ctx2d, qkv_vecs, attn_w_qkv[0].astype(bf), attn_q_norm[0].reshape(1, -1),
                   attn_k_norm[0].reshape(1, -1), cos_t, sin_t)
    o = _attention(q, k, v, S)
    vecs0 = _pad_rows([m_l[0][2], m_l[0][3], m_l[0][4], norm_ffn[0]])
    x1, hf, meta, gates, counts = _post_attn(x2d, o, attn_w_o[0].astype(bf), vecs0, rw[0], rb[0])
    x2 = _moe(x1, hf, meta, gates, counts, moe_w_gu[0].astype(bf), moe_b_gu[0], moe_w_down[0].astype(bf),
              moe_b_down[0], _pad_rows([m_l[0][5]]), final=False)

    mvecs = _pad_rows([m_l[1][0], m_l[1][1], norm_mix[1], pool_scale[0]])
    vecs1 = _pad_rows([m_l[1][2], m_l[1][3], m_l[1][4], norm_ffn[1]])
    x3, hf, meta, gates, counts = _pool(x2, mvecs, pool_w[0].astype(bf), vecs1, rw[1], rb[1])
    out = _moe(x3, hf, meta, gates, counts, moe_w_gu[1].astype(bf), moe_b_gu[1], moe_w_down[1].astype(bf),
               moe_b_down[1], _pad_rows([m_l[1][5], final_norm]), final=True)
    return out.reshape(B, S, D)
```

```python
import functools

import numpy as np
import jax
import jax.numpy as jnp
from jax import lax
from jax.experimental import pallas as pl
from jax.experimental.pallas import tpu as pltpu

D_MODEL = 1024
GRID_W = 64
N_HEADS = 8
N_KV_HEADS = 2
KV_GROUP = N_HEADS // N_KV_HEADS
HEAD_DIM = 128
ROPE_THETA = 10000.0
POOL_WINDOWS = (2, 4, 8, 16)
POOL_GROUP = D_MODEL // len(POOL_WINDOWS)
POOL_HALO = 8
N_EXPERTS = 32
TOP_K = 4
D_FF = D_MODEL
SWIGLU_LIMIT = 7.0
SWIGLU_ALPHA = 1.702
NORM_EPS = 1e-6
N_MOD = 6

LANES = 128
SUBLANES = 8
D_CHUNKS = D_MODEL // LANES
TM = 256
TK_MAX = 1280
VMEM_LIMIT = 48 * 1024 * 1024
ATTN_VMEM_LIMIT = 56 * 1024 * 1024
LOG2E = 1.4426950408889634
NEG_BIG = -1e30

_HI = lax.Precision.HIGHEST


def _cparams(sem):
    return pltpu.CompilerParams(dimension_semantics=sem, vmem_limit_bytes=VMEM_LIMIT)


def _rms(x, eps=NORM_EPS):
    return x * lax.rsqrt(jnp.mean(x * x, axis=-1, keepdims=True) + eps)


def _mod_kernel(cv_ref, w_ref, b_ref, o_ref):
    cv = cv_ref[...]
    s = cv / (1.0 + jnp.exp(-cv))
    o_ref[0] = jnp.dot(s, w_ref[0], precision=_HI, preferred_element_type=jnp.float32) + b_ref[0]


def _modulation(cvec, ada_w, ada_b):
    depth = ada_w.shape[0]
    nblk = 4
    bw = N_MOD * D_MODEL // nblk
    return pl.pallas_call(
        _mod_kernel,
        grid=(depth, nblk),
        in_specs=[
            pl.BlockSpec((SUBLANES, D_MODEL), lambda l, j: (0, 0)),
            pl.BlockSpec((1, D_MODEL, bw), lambda l, j: (l, 0, j)),
            pl.BlockSpec((1, 1, bw), lambda l, j: (l, 0, j)),
        ],
        out_specs=pl.BlockSpec((1, SUBLANES, bw), lambda l, j: (l, 0, j)),
        out_shape=jax.ShapeDtypeStruct((depth, SUBLANES, N_MOD * D_MODEL), jnp.float32),
        compiler_params=_cparams(("arbitrary", "arbitrary")),
        name="modulation",
    )(cvec, ada_w, ada_b.reshape(depth, 1, N_MOD * D_MODEL))


def _rope(x, cos, sin_signed):
    lane = lax.broadcasted_iota(jnp.int32, x.shape, 1)
    partner = jnp.where((lane % 64) < 32, pltpu.roll(x, LANES - 32, 1), pltpu.roll(x, 32, 1))
    return x * cos + partner * sin_signed


def _qkv_kernel(n_lat_tiles, x_ref, ctx_ref, vec_ref, w_ref, qg_ref, kg_ref, cos_ref, sin_ref,
                q_ref, k_ref, v_ref):
    i = pl.program_id(0)
    is_ctx = i >= n_lat_tiles
    xt = jnp.where(is_ctx, ctx_ref[...], x_ref[...])
    sh = jnp.where(is_ctx, vec_ref[2:3, :], vec_ref[0:1, :])
    sc = jnp.where(is_ctx, vec_ref[3:4, :], vec_ref[1:2, :])
    h = _rms(xt) * vec_ref[4:5, :] * (1.0 + sc) + sh
    qkv = jnp.dot(h.astype(jnp.bfloat16), w_ref[...], preferred_element_type=jnp.float32)
    cos = cos_ref[...]
    sin = sin_ref[...]
    q_scale = (HEAD_DIM ** -0.5) * LOG2E
    for hd in range(N_HEADS):
        qh = _rms(qkv[:, hd * HEAD_DIM:(hd + 1) * HEAD_DIM]) * qg_ref[...]
        q_ref[hd] = (_rope(qh, cos, sin) * q_scale).astype(jnp.bfloat16)
    for kv in range(N_KV_HEADS):
        c0 = (N_HEADS + kv) * HEAD_DIM
        kh = _rms(qkv[:, c0:c0 + HEAD_DIM]) * kg_ref[...]
        k_ref[kv] = _rope(kh, cos, sin).astype(jnp.bfloat16)
        c1 = (N_HEADS + N_KV_HEADS + kv) * HEAD_DIM
        v_ref[kv] = qkv[:, c1:c1 + HEAD_DIM].astype(jnp.bfloat16)


def _qkv(x, ctx, vecs, w_qkv, q_g, k_g, cos_t, sin_t):
    S, C = x.shape[0], ctx.shape[0]
    n_lat = S // TM
    n_ctx = C // TM
    T = S + C
    last = n_lat - 1
    qkv_w = w_qkv.shape[1]
    out_shapes = (
        jax.ShapeDtypeStruct((N_HEADS, T, HEAD_DIM), jnp.bfloat16),
        jax.ShapeDtypeStruct((N_KV_HEADS, T, HEAD_DIM), jnp.bfloat16),
        jax.ShapeDtypeStruct((N_KV_HEADS, T, HEAD_DIM), jnp.bfloat16),
    )
    return pl.pallas_call(
        functools.partial(_qkv_kernel, n_lat),
        grid=(n_lat + n_ctx,),
        in_specs=[
            pl.BlockSpec((TM, D_MODEL), lambda i: (jnp.minimum(i, last), 0)),
            pl.BlockSpec((TM, D_MODEL), lambda i: (jnp.maximum(i - n_lat, 0), 0)),
            pl.BlockSpec((SUBLANES, D_MODEL), lambda i: (0, 0)),
            pl.BlockSpec((D_MODEL, qkv_w), lambda i: (0, 0)),
            pl.BlockSpec((1, HEAD_DIM), lambda i: (0, 0)),
            pl.BlockSpec((1, HEAD_DIM), lambda i: (0, 0)),
            pl.BlockSpec((TM, HEAD_DIM), lambda i: (i, 0)),
            pl.BlockSpec((TM, HEAD_DIM), lambda i: (i, 0)),
        ],
        out_specs=(
            pl.BlockSpec((N_HEADS, TM, HEAD_DIM), lambda i: (0, i, 0)),
            pl.BlockSpec((N_KV_HEADS, TM, HEAD_DIM), lambda i: (0, i, 0)),
            pl.BlockSpec((N_KV_HEADS, TM, HEAD_DIM), lambda i: (0, i, 0)),
        ),
        out_shape=out_shapes,
        compiler_params=_cparams(("arbitrary",)),
        name="qkv_proj",
    )(x, ctx, vecs, w_qkv, q_g, k_g, cos_t, sin_t)


def _attn_kernel(n_chunks, tk, q_ref, k_ref, v_ref, o_ref, s_ref, p_ref, a_ref, mc_ref, m_ref, acc_ref):
    rows = KV_GROUP * TM
    ones = jnp.ones((tk, HEAD_DIM), jnp.bfloat16)

    def scores(c, slot):
        k = k_ref[0, pl.ds(pl.multiple_of(c * tk, tk), tk), :]
        q = q_ref[...].reshape(rows, HEAD_DIM)
        s = lax.dot_general(q, k, (((1,), (1,)), ((), ())), preferred_element_type=jnp.float32)
        s_ref[slot] = s
        mc_ref[slot] = jnp.broadcast_to(jnp.max(s, axis=1, keepdims=True), (rows, LANES))

    def softmax(slot):
        m_old = m_ref[...]
        m_new = jnp.maximum(m_old, mc_ref[slot])
        a_ref[slot] = jnp.exp2(m_old - m_new)
        m_ref[...] = m_new
        p_ref[slot] = jnp.exp2(s_ref[slot] - pltpu.repeat(m_new, tk // LANES, 1)).astype(jnp.bfloat16)

    def values(c, slot):
        v = v_ref[0, pl.ds(pl.multiple_of(c * tk, tk), tk), :]
        vx = jnp.concatenate([v, ones], axis=1)
        acc_ref[...] = (acc_ref[...] * pltpu.repeat(a_ref[slot], 2, 1)
                        + jnp.dot(p_ref[slot], vx, preferred_element_type=jnp.float32))

    def iteration(t, par):
        scores(t + 1, 1 - par)
        softmax(par)
        values(jnp.maximum(t - 1, 0), 1 - par)

    m_ref[...] = jnp.full(m_ref.shape, -jnp.inf, jnp.float32)
    acc_ref[...] = jnp.zeros(acc_ref.shape, jnp.float32)
    p_ref[1] = jnp.zeros(p_ref.shape[1:], jnp.bfloat16)
    a_ref[1] = jnp.ones(a_ref.shape[1:], jnp.float32)
    scores(0, 0)

    n_steady = n_chunks - 1

    def pair(u, carry):
        iteration(2 * u, 0)
        iteration(2 * u + 1, 1)
        return carry

    lax.fori_loop(0, n_steady // 2, pair, 0)
    if n_steady % 2:
        iteration(n_steady - 1, (n_steady - 1) % 2)
    last = n_chunks - 1
    softmax(last % 2)
    values(max(last - 1, 0), 1 - last % 2)
    values(last, last % 2)

    acc = acc_ref[...]
    o = acc[:, :HEAD_DIM] / acc[:, HEAD_DIM:]
    for g in range(KV_GROUP):
        o_ref[:, g * HEAD_DIM:(g + 1) * HEAD_DIM] = o[g * TM:(g + 1) * TM].astype(o_ref.dtype)


def _key_chunk(T):
    return max(c for c in range(LANES, TK_MAX + 1, LANES) if T % c == 0)


def _attention(q, k, v, S):
    T = k.shape[1]
    tk = _key_chunk(T)
    rows = KV_GROUP * TM
    return pl.pallas_call(
        functools.partial(_attn_kernel, T // tk, tk),
        grid=(N_KV_HEADS, S // TM),
        in_specs=[
            pl.BlockSpec((KV_GROUP, TM, HEAD_DIM), lambda g, i: (g, i, 0)),
            pl.BlockSpec((1, T, HEAD_DIM), lambda g, i: (g, 0, 0)),
            pl.BlockSpec((1, T, HEAD_DIM), lambda g, i: (g, 0, 0)),
        ],
        out_specs=pl.BlockSpec((TM, KV_GROUP * HEAD_DIM), lambda g, i: (i, g)),
        out_shape=jax.ShapeDtypeStruct((S, N_HEADS * HEAD_DIM), jnp.bfloat16),
        scratch_shapes=[
            pltpu.VMEM((2, rows, tk), jnp.float32),
            pltpu.VMEM((2, rows, tk), jnp.bfloat16),
            pltpu.VMEM((2, rows, LANES), jnp.float32),
            pltpu.VMEM((2, rows, LANES), jnp.float32),
            pltpu.VMEM((rows, LANES), jnp.float32),
            pltpu.VMEM((rows, 2 * HEAD_DIM), jnp.float32),
        ],
        compiler_params=pltpu.CompilerParams(dimension_semantics=("arbitrary", "arbitrary"),
                                             vmem_limit_bytes=ATTN_VMEM_LIMIT),
        name="flash_attention",
    )(q, k, v)


def _router_out_shapes(S):
    return (
        jax.ShapeDtypeStruct((S, D_MODEL), jnp.float32),
        jax.ShapeDtypeStruct((S * SUBLANES, LANES), jnp.float32),
        jax.ShapeDtypeStruct((S, LANES), jnp.int32),
        jax.ShapeDtypeStruct((S, LANES), jnp.float32),
        jax.ShapeDtypeStruct((SUBLANES, LANES), jnp.int32),
    )


def _router_out_specs():
    return (
        pl.BlockSpec((TM, D_MODEL), lambda i: (i, 0)),
        pl.BlockSpec((TM * SUBLANES, LANES), lambda i: (i, 0)),
        pl.BlockSpec((TM, LANES), lambda i: (i, 0)),
        pl.BlockSpec((TM, LANES), lambda i: (i, 0)),
        pl.BlockSpec((SUBLANES, LANES), lambda i: (0, 0)),
    )


def _router_core(x_new, vec_ref, rw_ref, rb_ref, cnt_ref, x_out_ref, hf_ref, meta_ref, gate_ref,
                 counts_ref):
    i = pl.program_id(0)

    @pl.when(i == 0)
    def _():
        cnt_ref[...] = jnp.zeros_like(cnt_ref)

    x_out_ref[...] = x_new
    hf = _rms(x_new) * vec_ref[3:4, :] * (1.0 + vec_ref[2:3, :]) + vec_ref[1:2, :]
    for c in range(D_CHUNKS):
        hf_ref[pl.ds(c, TM, stride=SUBLANES), :] = hf[:, c * LANES:(c + 1) * LANES]

    logits = jnp.dot(hf, rw_ref[...], precision=_HI, preferred_element_type=jnp.float32) + rb_ref[...]
    lane = lax.broadcasted_iota(jnp.int32, logits.shape, 1)
    lane_f = lane.astype(jnp.float32)
    vals, idxs, hots = [], [], []
    work = logits
    for _ in range(TOP_K):
        mx = jnp.max(work, axis=1, keepdims=True)
        idx = jnp.min(jnp.where(work == mx, lane_f, float(LANES)), axis=1, keepdims=True)
        hot = lane_f == idx
        work = jnp.where(hot, -jnp.inf, work)
        vals.append(mx)
        idxs.append(idx.astype(jnp.int32))
        hots.append(hot)
    exps = [jnp.exp(vk - vals[0]) for vk in vals]
    denom = exps[0] + exps[1] + exps[2] + exps[3]

    any_hot = hots[0] | hots[1] | hots[2] | hots[3]
    hot_f = jnp.where(any_hot, 1.0, 0.0)
    r_i = lax.broadcasted_iota(jnp.int32, (TM, TM), 0)
    c_i = lax.broadcasted_iota(jnp.int32, (TM, TM), 1)
    lower = jnp.where(c_i < r_i, 1.0, 0.0).astype(jnp.bfloat16)
    before = jnp.dot(lower, hot_f.astype(jnp.bfloat16), preferred_element_type=jnp.float32) + cnt_ref[0:1, :]
    cnt_ref[0:1, :] = cnt_ref[0:1, :] + jnp.sum(hot_f, axis=0, keepdims=True)

    meta = jnp.zeros(logits.shape, jnp.int32)
    gates = jnp.zeros(logits.shape, jnp.float32)
    for k in range(TOP_K):
        rank = jnp.sum(jnp.where(hots[k], before, 0.0), axis=1, keepdims=True).astype(jnp.int32)
        meta = jnp.where(lane == k, idxs[k], meta)
        meta = jnp.where(lane == TOP_K + k, rank, meta)
        gates = jnp.where(lane == k, exps[k] / denom, gates)
    meta_ref[...] = meta
    gate_ref[...] = gates
    counts_ref[...] = jnp.broadcast_to(cnt_ref[0:1, :], counts_ref.shape).astype(jnp.int32)


def _post_attn_kernel(x_ref, o_ref, wo_ref, vec_ref, rw_ref, rb_ref,
                      x_out_ref, hf_ref, meta_ref, gate_ref, counts_ref, cnt_ref):
    y = jnp.dot(o_ref[...], wo_ref[...], preferred_element_type=jnp.float32)
    x_new = x_ref[...] + vec_ref[0:1, :] * y
    _router_core(x_new, vec_ref, rw_ref, rb_ref, cnt_ref, x_out_ref, hf_ref, meta_ref, gate_ref,
                 counts_ref)


def _post_attn(x, o, w_o, vecs, rw, rb):
    S = x.shape[0]
    return pl.pallas_call(
        _post_attn_kernel,
        grid=(S // TM,),
        in_specs=[
            pl.BlockSpec((TM, D_MODEL), lambda i: (i, 0)),
            pl.BlockSpec((TM, D_MODEL), lambda i: (i, 0)),
            pl.BlockSpec((D_MODEL, D_MODEL), lambda i: (0, 0)),
            pl.BlockSpec((SUBLANES, D_MODEL), lambda i: (0, 0)),
            pl.BlockSpec((D_MODEL, LANES), lambda i: (0, 0)),
            pl.BlockSpec((1, LANES), lambda i: (0, 0)),
        ],
        out_specs=_router_out_specs(),
        out_shape=_router_out_shapes(S),
        scratch_shapes=[pltpu.VMEM((SUBLANES, LANES), jnp.float32)],
        compiler_params=_cparams(("arbitrary",)),
        name="post_attn_router",
    )(x, o, w_o, vecs, rw, rb)


def _pool_kernel(S, x_ref, prev_ref, next_ref, mvec_ref, pw_ref, vec_ref, rw_ref, rb_ref,
                 x_out_ref, hf_ref, meta_ref, gate_ref, counts_ref, cnt_ref, buf_ref):
    i = pl.program_id(0)
    n = pl.num_programs(0)

    def hmod(xx):
        return _rms(xx) * mvec_ref[2:3, :] * (1.0 + mvec_ref[1:2, :]) + mvec_ref[0:1, :]

    x_cur = x_ref[...]
    h_cur = hmod(x_cur)
    buf_ref[0:POOL_HALO, :] = jnp.where(i > 0, hmod(prev_ref[...]), 0.0)
    buf_ref[POOL_HALO:POOL_HALO + TM, :] = h_cur
    buf_ref[POOL_HALO + TM:, :] = jnp.where(i < n - 1, hmod(next_ref[...]), 0.0)

    t = i * TM + lax.broadcasted_iota(jnp.int32, (TM, 1), 0)
    ys = []
    for g, w in enumerate(POOL_WINDOWS):
        c0 = g * POOL_GROUP
        acc = None
        for d in range(-(w // 2), w - w // 2):
            piece = buf_ref[POOL_HALO + d:POOL_HALO + d + TM, c0:c0 + POOL_GROUP]
            acc = piece if acc is None else acc + piece
        cnt = jnp.minimum(t - w // 2 + w, S) - jnp.maximum(t - w // 2, 0)
        pooled = acc / cnt.astype(jnp.float32) - h_cur[:, c0:c0 + POOL_GROUP]
        ys.append(jnp.dot(pooled.astype(jnp.bfloat16), pw_ref[g], preferred_element_type=jnp.float32))
    y = jnp.concatenate(ys, axis=1) * mvec_ref[3:4, :]
    x_new = x_cur + vec_ref[0:1, :] * y
    _router_core(x_new, vec_ref, rw_ref, rb_ref, cnt_ref, x_out_ref, hf_ref, meta_ref, gate_ref,
                 counts_ref)


def _pool(x, mvecs, pool_w, vecs, rw, rb):
    S = x.shape[0]
    hb = TM // POOL_HALO
    n_halo = S // POOL_HALO
    return pl.pallas_call(
        functools.partial(_pool_kernel, S),
        grid=(S // TM,),
        in_specs=[
            pl.BlockSpec((TM, D_MODEL), lambda i: (i, 0)),
            pl.BlockSpec((POOL_HALO, D_MODEL), lambda i: (jnp.maximum(i * hb - 1, 0), 0)),
            pl.BlockSpec((POOL_HALO, D_MODEL), lambda i: (jnp.minimum((i + 1) * hb, n_halo - 1), 0)),
            pl.BlockSpec((SUBLANES, D_MODEL), lambda i: (0, 0)),
            pl.BlockSpec((len(POOL_WINDOWS), POOL_GROUP, POOL_GROUP), lambda i: (0, 0, 0)),
            pl.BlockSpec((SUBLANES, D_MODEL), lambda i: (0, 0)),
            pl.BlockSpec((D_MODEL, LANES), lambda i: (0, 0)),
            pl.BlockSpec((1, LANES), lambda i: (0, 0)),
        ],
        out_specs=_router_out_specs(),
        out_shape=_router_out_shapes(S),
        scratch_shapes=[
            pltpu.VMEM((SUBLANES, LANES), jnp.float32),
            pltpu.VMEM((TM + 2 * POOL_HALO, D_MODEL), jnp.float32),
        ],
        compiler_params=_cparams(("arbitrary",)),
        name="pool_router",
    )(x, x, x, mvecs, pool_w, vecs, rw, rb)


def _dispatch_kernel(pos_ref, ztile_ref, hf_ref, xs_ref, stage_ref, zeros_ref, load_sem, row_sem, zero_sem):
    i = pl.program_id(0)
    n = pl.num_programs(0)
    tile_rows = TM * SUBLANES
    pairs = TM * TOP_K

    def load(tile, slot):
        src = pl.multiple_of(tile * tile_rows, tile_rows)
        dst = pl.multiple_of(slot * tile_rows, tile_rows)
        return pltpu.make_async_copy(hf_ref.at[pl.ds(src, tile_rows)],
                                     stage_ref.at[pl.ds(dst, tile_rows)], load_sem.at[slot])

    def rows_wait():
        pltpu.make_async_copy(xs_ref.at[pl.ds(0, pairs * SUBLANES)],
                              xs_ref.at[pl.ds(0, pairs * SUBLANES)], row_sem).wait()

    def zero_copy(e):
        start = pl.multiple_of(jnp.maximum(ztile_ref[e], 0) * tile_rows, tile_rows)
        return pltpu.make_async_copy(zeros_ref, xs_ref.at[pl.ds(start, tile_rows)], zero_sem)

    @pl.when(i == 0)
    def _():
        zeros_ref[...] = jnp.zeros_like(zeros_ref)
        load(0, 0).start()

        def zero_start(e, carry):
            @pl.when(ztile_ref[e] >= 0)
            def _():
                zero_copy(e).start()
            return carry

        def zero_wait(e, carry):
            @pl.when(ztile_ref[e] >= 0)
            def _():
                zero_copy(e).wait()
            return carry

        lax.fori_loop(0, 2 * N_EXPERTS, zero_start, 0)
        lax.fori_loop(0, 2 * N_EXPERTS, zero_wait, 0)

    @pl.when(i > 0)
    def _():
        rows_wait()

    @pl.when(i + 1 < n)
    def _():
        load(i + 1, (i + 1) % 2).start()

    slot = i % 2
    load(i, slot).wait()

    def issue(r, carry):
        src = pl.multiple_of(slot * tile_rows + r * SUBLANES, SUBLANES)
        for k in range(TOP_K):
            dst = pl.multiple_of(pos_ref[(i * TM + r) * TOP_K + k] * SUBLANES, SUBLANES)
            pltpu.make_async_copy(stage_ref.at[pl.ds(src, SUBLANES)],
                                  xs_ref.at[pl.ds(dst, SUBLANES)], row_sem).start(priority=k % 2)
        return carry

    lax.fori_loop(0, TM, issue, 0)

    @pl.when(i == n - 1)
    def _():
        rows_wait()


def _dispatch(pos, ztile, hf_tiles, n_rows):
    S = hf_tiles.shape[0] // SUBLANES
    tile_rows = TM * SUBLANES
    return pl.pallas_call(
        _dispatch_kernel,
        grid_spec=pltpu.PrefetchScalarGridSpec(
            num_scalar_prefetch=2,
            grid=(S // TM,),
            in_specs=[pl.BlockSpec(memory_space=pl.ANY)],
            out_specs=pl.BlockSpec(memory_space=pl.ANY),
            scratch_shapes=[
                pltpu.VMEM((2 * tile_rows, LANES), jnp.float32),
                pltpu.VMEM((tile_rows, LANES), jnp.float32),
                pltpu.SemaphoreType.DMA((2,)),
                pltpu.SemaphoreType.DMA(()),
                pltpu.SemaphoreType.DMA(()),
            ],
        ),
        out_shape=jax.ShapeDtypeStruct((n_rows * SUBLANES, LANES), jnp.float32),
        compiler_params=_cparams(("arbitrary",)),
        name="moe_dispatch",
    )(pos, ztile, hf_tiles)


W_CAST_ROWS = 64


def _expert_kernel(te_ref, nused_ref, xs_ref, wgu_ref, bgu_ref, wd_ref, bd_ref, ys_ref,
                   wgu_bf_ref, wd_bf_ref):
    i = pl.program_id(0)
    used = i < nused_ref[0]
    new_expert = (i == 0) | (te_ref[i] != te_ref[jnp.maximum(i - 1, 0)])

    @pl.when(used & new_expert)
    def _():
        def cast_gu(r, carry):
            rows = pl.ds(pl.multiple_of(r * W_CAST_ROWS, W_CAST_ROWS), W_CAST_ROWS)
            wgu_bf_ref[rows, :] = wgu_ref[rows, :].astype(jnp.bfloat16)
            return carry

        def cast_d(r, carry):
            rows = pl.ds(pl.multiple_of(r * W_CAST_ROWS, W_CAST_ROWS), W_CAST_ROWS)
            wd_bf_ref[rows, :] = wd_ref[rows, :].astype(jnp.bfloat16)
            return carry

        lax.fori_loop(0, D_MODEL // W_CAST_ROWS, cast_gu, 0)
        lax.fori_loop(0, D_FF // W_CAST_ROWS, cast_d, 0)

    @pl.when(used)
    def _():
        cols = [xs_ref[pl.ds(c, TM, stride=SUBLANES), :] for c in range(D_CHUNKS)]
        x = jnp.concatenate(cols, axis=1).astype(jnp.bfloat16)
        gu = jnp.dot(x, wgu_bf_ref[...], preferred_element_type=jnp.float32) + bgu_ref[...]
        gate = jnp.minimum(gu[:, :D_FF], SWIGLU_LIMIT)
        up = jnp.clip(gu[:, D_FF:], -SWIGLU_LIMIT, SWIGLU_LIMIT)
        glu = gate / (1.0 + jnp.exp(-SWIGLU_ALPHA * gate))
        a = ((up + 1.0) * glu).astype(jnp.bfloat16)
        y = jnp.dot(a, wd_bf_ref[...], preferred_element_type=jnp.float32) + bd_ref[...]
        for c in range(D_CHUNKS):
            ys_ref[pl.ds(c, TM, stride=SUBLANES), :] = y[:, c * LANES:(c + 1) * LANES]

    @pl.when(jnp.logical_not(used))
    def _():
        ys_ref[...] = jnp.zeros_like(ys_ref)


def _experts(layer, tile_expert, n_used, xs, w_gu, b_gu, w_down, b_down):
    n_tiles = tile_expert.shape[0]
    L, E = w_gu.shape[:2]

    def row_map(i, te, nu):
        return (jnp.minimum(i, nu[0] - 1), 0)

    def w_map(i, te, nu):
        return (layer, te[i], 0, 0)

    return pl.pallas_call(
        _expert_kernel,
        grid_spec=pltpu.PrefetchScalarGridSpec(
            num_scalar_prefetch=2,
            grid=(n_tiles,),
            in_specs=[
                pl.BlockSpec((TM * SUBLANES, LANES), row_map),
                pl.BlockSpec((None, None, D_MODEL, 2 * D_FF), w_map),
                pl.BlockSpec((None, None, 1, 2 * D_FF), w_map),
                pl.BlockSpec((None, None, D_FF, D_MODEL), w_map),
                pl.BlockSpec((None, None, 1, D_MODEL), w_map),
            ],
            out_specs=pl.BlockSpec((TM * SUBLANES, LANES), lambda i, te, nu: (i, 0)),
            scratch_shapes=[
                pltpu.VMEM((D_MODEL, 2 * D_FF), jnp.bfloat16),
                pltpu.VMEM((D_FF, D_MODEL), jnp.bfloat16),
            ],
        ),
        out_shape=jax.ShapeDtypeStruct(xs.shape, jnp.float32),
        compiler_params=_cparams(("arbitrary",)),
        name="moe_experts",
    )(tile_expert, n_used, xs, w_gu, b_gu.reshape(L, E, 1, 2 * D_FF), w_down,
      b_down.reshape(L, E, 1, D_MODEL))


def _combine_kernel(final, pos_ref, ys_ref, x_ref, gate_ref, vec_ref, o_ref, buf_ref, sem):
    i = pl.program_id(0)
    n = pl.num_programs(0)
    slot_rows = TOP_K * TM * SUBLANES

    def issue(tile, slot):
        def one(r, carry):
            for k in range(TOP_K):
                src = pl.multiple_of(pos_ref[(tile * TM + r) * TOP_K + k] * SUBLANES, SUBLANES)
                dst = pl.multiple_of(slot * slot_rows + (k * TM + r) * SUBLANES, SUBLANES)
                pltpu.make_async_copy(ys_ref.at[pl.ds(src, SUBLANES)],
                                      buf_ref.at[pl.ds(dst, SUBLANES)], sem.at[slot]).start(priority=k % 2)
            return carry

        lax.fori_loop(0, TM, one, 0)

    @pl.when(i == 0)
    def _():
        issue(0, 0)

    @pl.when(i + 1 < n)
    def _():
        issue(i + 1, (i + 1) % 2)

    slot = i % 2
    base = pl.multiple_of(slot * slot_rows, slot_rows)
    pltpu.make_async_copy(ys_ref.at[pl.ds(0, slot_rows)], buf_ref.at[pl.ds(base, slot_rows)],
                          sem.at[slot]).wait()

    gates = gate_ref[...]
    pieces = []
    for c in range(D_CHUNKS):
        f = None
        for k in range(TOP_K):
            rows = buf_ref[pl.ds(base + k * TM * SUBLANES + c, TM, stride=SUBLANES), :]
            term = gates[:, k:k + 1] * rows
            f = term if f is None else f + term
        pieces.append(f)
    f = jnp.concatenate(pieces, axis=1)
    x_new = x_ref[...] + vec_ref[0:1, :] * f
    if final:
        x_new = _rms(x_new) * vec_ref[1:2, :]
    o_ref[...] = x_new


def _combine(pos, ys, x, gates, vecs, final):
    S = x.shape[0]
    return pl.pallas_call(
        functools.partial(_combine_kernel, final),
        grid_spec=pltpu.PrefetchScalarGridSpec(
            num_scalar_prefetch=1,
            grid=(S // TM,),
            in_specs=[
                pl.BlockSpec(memory_space=pl.ANY),
                pl.BlockSpec((TM, D_MODEL), lambda i, p: (i, 0)),
                pl.BlockSpec((TM, LANES), lambda i, p: (i, 0)),
                pl.BlockSpec((SUBLANES, D_MODEL), lambda i, p: (0, 0)),
            ],
            out_specs=pl.BlockSpec((TM, D_MODEL), lambda i, p: (i, 0)),
            scratch_shapes=[
                pltpu.VMEM((2 * TOP_K * TM * SUBLANES, LANES), jnp.float32),
                pltpu.SemaphoreType.DMA((2,)),
            ],
        ),
        out_shape=jax.ShapeDtypeStruct((S, D_MODEL), jnp.float32),
        compiler_params=_cparams(("arbitrary",)),
        name="moe_combine",
    )(pos, ys, x, gates, vecs)


def _moe(layer, x_new, hf_tiles, meta, gates, counts, w_gu, b_gu, w_down, b_down, out_vecs, final):
    S = x_new.shape[0]
    n_tiles = (S * TOP_K) // TM + N_EXPERTS
    cnt = counts[0, :N_EXPERTS]
    padded = ((cnt + TM - 1) // TM) * TM
    ends = jnp.cumsum(padded)
    offs = ends - padded
    eid = meta[:, :TOP_K]
    rank = meta[:, TOP_K:2 * TOP_K]
    pos = (offs[eid] + rank).reshape(-1).astype(jnp.int32)
    tile_ends = ends // TM
    tile_ids = jnp.arange(n_tiles, dtype=jnp.int32)
    tile_expert = jnp.minimum(
        jnp.sum((tile_ends[None, :] <= tile_ids[:, None]).astype(jnp.int32), axis=1),
        N_EXPERTS - 1).astype(jnp.int32)
    n_used = tile_ends[-1:].astype(jnp.int32)
    trailing = tile_ends[-1] + jnp.arange(N_EXPERTS, dtype=jnp.int32)
    ztile = jnp.concatenate([
        jnp.where(padded > 0, tile_ends - 1, -1),
        jnp.where(trailing < n_tiles, trailing, -1)]).astype(jnp.int32)
    xs = _dispatch(pos, ztile, hf_tiles, n_tiles * TM)
    ys = _experts(layer, tile_expert, n_used, xs, w_gu, b_gu, w_down, b_down)
    return _combine(pos, ys, x_new, gates, out_vecs, final)


def _pad_rows(rows):
    rows = [r.reshape(1, -1) for r in rows]
    rows += [jnp.zeros_like(rows[0])] * (SUBLANES - len(rows))
    return jnp.concatenate(rows, axis=0)


def _rope_tables(S, C):
    quarter = HEAD_DIM // 4
    inv_freq = ROPE_THETA ** (-np.arange(quarter, dtype=np.float32) / quarter)
    n_rows = S // GRID_W
    ang_r = np.arange(n_rows, dtype=np.float32)[:, None] * inv_freq[None, :]
    ang_c = np.arange(GRID_W, dtype=np.float32)[:, None] * inv_freq[None, :]

    def expand(tab_r, tab_c, sign):
        r = jnp.repeat(jnp.asarray(tab_r, jnp.float32), GRID_W, axis=0)
        cc = jnp.tile(jnp.asarray(tab_c, jnp.float32), (n_rows, 1))
        return jnp.concatenate([sign * r, r, sign * cc, cc], axis=1)

    cos = expand(np.cos(ang_r), np.cos(ang_c), 1.0)
    sin = expand(np.sin(ang_r), np.sin(ang_c), -1.0)
    cos = jnp.concatenate([cos, jnp.ones((C, HEAD_DIM), jnp.float32)], axis=0)
    sin = jnp.concatenate([sin, jnp.zeros((C, HEAD_DIM), jnp.float32)], axis=0)
    return cos, sin


def kernel(x, c, ctx, c_ctx, ada_w, ada_b, norm_mix, norm_ffn, attn_w_qkv, attn_q_norm, attn_k_norm,
           attn_w_o, pool_w, pool_scale, moe_router_w, moe_router_b, moe_w_gu, moe_b_gu, moe_w_down,
           moe_b_down, final_norm):
    B, S, D = x.shape
    C = ctx.shape[1]
    assert B == 1 and D == D_MODEL and S % TM == 0 and C % TM == 0 and S % GRID_W == 0
    x2d = x.reshape(S, D)
    ctx2d = ctx.reshape(C, D)
    bf = jnp.bfloat16

    cvec = _pad_rows([c.reshape(-1), c_ctx])
    mod = _modulation(cvec, ada_w, ada_b)
    m_l = [mod[l, 0].reshape(N_MOD, D) for l in range(2)]
    m_c0 = mod[0, 1].reshape(N_MOD, D)

    rw = [jnp.pad(moe_router_w[l], ((0, 0), (0, LANES - N_EXPERTS))) for l in range(2)]
    rb = [jnp.pad(moe_router_b[l], (0, LANES - N_EXPERTS), constant_values=NEG_BIG).reshape(1, LANES)
          for l in range(2)]

    cos_t, sin_t = _rope_tables(S, C)
    qkv_vecs = _pad_rows([m_l[0][0], m_l[0][1], m_c0[0], m_c0[1], norm_mix[0]])
    q, k, v = _qkv(x2d, ctx2d, qkv_vecs, attn_w_qkv[0].astype(bf), attn_q_norm[0].reshape(1, -1),
                   attn_k_norm[0].reshape(1, -1), cos_t, sin_t)
    o = _attention(q, k, v, S)
    vecs0 = _pad_rows([m_l[0][2], m_l[0][3], m_l[0][4], norm_ffn[0]])
    x1, hf, meta, gates, counts = _post_attn(x2d, o, attn_w_o[0].astype(bf), vecs0, rw[0], rb[0])
    x2 = _moe(0, x1, hf, meta, gates, counts, moe_w_gu, moe_b_gu, moe_w_down, moe_b_down,
              _pad_rows([m_l[0][5]]), final=False)

    mvecs = _pad_rows([m_l[1][0], m_l[1][1], norm_mix[1], pool_scale[0]])
    vecs1 = _pad_rows([m_l[1][2], m_l[1][3], m_l[1][4], norm_ffn[1]])
    x3, hf, meta, gates, counts = _pool(x2, mvecs, pool_w[0].astype(bf), vecs1, rw[1], rb[1])
    out = _moe(1, x3, hf, meta, gates, counts, moe_w_gu, moe_b_gu, moe_w_down, moe_b_down,
               _pad_rows([m_l[1][5], final_norm]), final=True)
    return out.reshape(B, S, D)
```

```python
import functools

import numpy as np
import jax
import jax.numpy as jnp
from jax import lax
from jax.experimental import pallas as pl
from jax.experimental.pallas import tpu as pltpu

D_MODEL = 1024
GRID_W = 64
N_HEADS = 8
N_KV_HEADS = 2
KV_GROUP = N_HEADS // N_KV_HEADS
HEAD_DIM = 128
ROPE_THETA = 10000.0
POOL_WINDOWS = (2, 4, 8, 16)
POOL_GROUP = D_MODEL // len(POOL_WINDOWS)
POOL_HALO = 8
N_EXPERTS = 32
TOP_K = 4
D_FF = D_MODEL
SWIGLU_LIMIT = 7.0
SWIGLU_ALPHA = 1.702
NORM_EPS = 1e-6
N_MOD = 6

LANES = 128
SUBLANES = 8
D_CHUNKS = D_MODEL // LANES
TM = 256
TK_MAX = 1280
ATTN_ROW_SUM_FLOOR = 2.0 ** -100
ATTN_UNROLL = 2
VMEM_LIMIT = 48 * 1024 * 1024
ATTN_VMEM_LIMIT = 56 * 1024 * 1024
LOG2E = 1.4426950408889634
NEG_BIG = -1e30

_HI = lax.Precision.HIGHEST


def _cparams(sem):
    return pltpu.CompilerParams(dimension_semantics=sem, vmem_limit_bytes=VMEM_LIMIT)


def _rms(x, eps=NORM_EPS):
    return x * lax.rsqrt(jnp.mean(x * x, axis=-1, keepdims=True) + eps)


def _mod_kernel(cv_ref, w_ref, b_ref, o_ref):
    cv = cv_ref[...]
    s = cv / (1.0 + jnp.exp(-cv))
    o_ref[0] = jnp.dot(s, w_ref[0], precision=_HI, preferred_element_type=jnp.float32) + b_ref[0]


def _modulation(cvec, ada_w, ada_b):
    depth = ada_w.shape[0]
    nblk = 4
    bw = N_MOD * D_MODEL // nblk
    return pl.pallas_call(
        _mod_kernel,
        grid=(depth, nblk),
        in_specs=[
            pl.BlockSpec((SUBLANES, D_MODEL), lambda l, j: (0, 0)),
            pl.BlockSpec((1, D_MODEL, bw), lambda l, j: (l, 0, j)),
            pl.BlockSpec((1, 1, bw), lambda l, j: (l, 0, j)),
        ],
        out_specs=pl.BlockSpec((1, SUBLANES, bw), lambda l, j: (l, 0, j)),
        out_shape=jax.ShapeDtypeStruct((depth, SUBLANES, N_MOD * D_MODEL), jnp.float32),
        compiler_params=_cparams(("arbitrary", "arbitrary")),
        name="modulation",
    )(cvec, ada_w, ada_b.reshape(depth, 1, N_MOD * D_MODEL))


def _rope(x, cos, sin_signed):
    lane = lax.broadcasted_iota(jnp.int32, x.shape, 1)
    partner = jnp.where((lane % 64) < 32, pltpu.roll(x, LANES - 32, 1), pltpu.roll(x, 32, 1))
    return x * cos + partner * sin_signed


def _qkv_kernel(n_lat_tiles, x_ref, ctx_ref, vec_ref, w_ref, qg_ref, kg_ref, cos_ref, sin_ref,
                q_ref, k_ref, v_ref):
    i = pl.program_id(0)
    is_ctx = i >= n_lat_tiles
    xt = jnp.where(is_ctx, ctx_ref[...], x_ref[...])
    sh = jnp.where(is_ctx, vec_ref[2:3, :], vec_ref[0:1, :])
    sc = jnp.where(is_ctx, vec_ref[3:4, :], vec_ref[1:2, :])
    h = _rms(xt) * vec_ref[4:5, :] * (1.0 + sc) + sh
    qkv = jnp.dot(h.astype(jnp.bfloat16), w_ref[...], preferred_element_type=jnp.float32)
    cos = cos_ref[...]
    sin = sin_ref[...]
    q_scale = (HEAD_DIM ** -0.5) * LOG2E
    for hd in range(N_HEADS):
        qh = _rms(qkv[:, hd * HEAD_DIM:(hd + 1) * HEAD_DIM]) * qg_ref[...]
        q_ref[hd] = (_rope(qh, cos, sin) * q_scale).astype(jnp.bfloat16)
    for kv in range(N_KV_HEADS):
        c0 = (N_HEADS + kv) * HEAD_DIM
        kh = _rms(qkv[:, c0:c0 + HEAD_DIM]) * kg_ref[...]
        k_ref[kv] = _rope(kh, cos, sin).astype(jnp.bfloat16)
        c1 = (N_HEADS + N_KV_HEADS + kv) * HEAD_DIM
        v_ref[kv] = qkv[:, c1:c1 + HEAD_DIM].astype(jnp.bfloat16)


def _qkv(x, ctx, vecs, w_qkv, q_g, k_g, cos_t, sin_t):
    S, C = x.shape[0], ctx.shape[0]
    n_lat = S // TM
    n_ctx = C // TM
    T = S + C
    last = n_lat - 1
    qkv_w = w_qkv.shape[1]
    out_shapes = (
        jax.ShapeDtypeStruct((N_HEADS, T, HEAD_DIM), jnp.bfloat16),
        jax.ShapeDtypeStruct((N_KV_HEADS, T, HEAD_DIM), jnp.bfloat16),
        jax.ShapeDtypeStruct((N_KV_HEADS, T, HEAD_DIM), jnp.bfloat16),
    )
    return pl.pallas_call(
        functools.partial(_qkv_kernel, n_lat),
        grid=(n_lat + n_ctx,),
        in_specs=[
            pl.BlockSpec((TM, D_MODEL), lambda i: (jnp.minimum(i, last), 0)),
            pl.BlockSpec((TM, D_MODEL), lambda i: (jnp.maximum(i - n_lat, 0), 0)),
            pl.BlockSpec((SUBLANES, D_MODEL), lambda i: (0, 0)),
            pl.BlockSpec((D_MODEL, qkv_w), lambda i: (0, 0)),
            pl.BlockSpec((1, HEAD_DIM), lambda i: (0, 0)),
            pl.BlockSpec((1, HEAD_DIM), lambda i: (0, 0)),
            pl.BlockSpec((TM, HEAD_DIM), lambda i: (i, 0)),
            pl.BlockSpec((TM, HEAD_DIM), lambda i: (i, 0)),
        ],
        out_specs=(
            pl.BlockSpec((N_HEADS, TM, HEAD_DIM), lambda i: (0, i, 0)),
            pl.BlockSpec((N_KV_HEADS, TM, HEAD_DIM), lambda i: (0, i, 0)),
            pl.BlockSpec((N_KV_HEADS, TM, HEAD_DIM), lambda i: (0, i, 0)),
        ),
        out_shape=out_shapes,
        compiler_params=_cparams(("arbitrary",)),
        name="qkv_proj",
    )(x, ctx, vecs, w_qkv, q_g, k_g, cos_t, sin_t)


def _lane_repeat(x, n):
    return jnp.concatenate([x] * n, axis=1)


def _attn_kernel(n_chunks, tk, q_ref, k_ref, v_ref, o_ref, s_ref, p_ref, a_ref, mc_ref, m_ref, acc_ref):
    rows = KV_GROUP * TM
    ones = jnp.ones((tk, HEAD_DIM), jnp.bfloat16)

    def scores(c, slot):
        k = k_ref[0, pl.ds(pl.multiple_of(c * tk, tk), tk), :]
        q = q_ref[...].reshape(rows, HEAD_DIM)
        s = lax.dot_general(q, k, (((1,), (1,)), ((), ())), preferred_element_type=jnp.float32)
        s_ref[slot] = s
        mc_ref[slot] = jnp.broadcast_to(jnp.max(s, axis=1, keepdims=True), (rows, LANES))

    def softmax(slot):
        m_old = m_ref[...]
        m_new = jnp.maximum(m_old, mc_ref[slot])
        a_ref[slot] = jnp.exp2(m_old - m_new)
        m_ref[...] = m_new
        p_ref[slot] = jnp.exp2(s_ref[slot] - _lane_repeat(m_new, tk // LANES)).astype(jnp.bfloat16)

    def values(c, slot):
        v = v_ref[0, pl.ds(pl.multiple_of(c * tk, tk), tk), :]
        vx = jnp.concatenate([v, ones], axis=1)
        acc_ref[...] = (acc_ref[...] * _lane_repeat(a_ref[slot], 2)
                        + jnp.dot(p_ref[slot], vx, preferred_element_type=jnp.float32))

    def iteration(t, par):
        scores(t + 1, 1 - par)
        softmax(par)
        values(jnp.maximum(t - 1, 0), 1 - par)

    m_ref[...] = jnp.full(m_ref.shape, -jnp.inf, jnp.float32)
    acc_ref[...] = jnp.zeros(acc_ref.shape, jnp.float32)
    p_ref[1] = jnp.zeros(p_ref.shape[1:], jnp.bfloat16)
    a_ref[1] = jnp.ones(a_ref.shape[1:], jnp.float32)
    scores(0, 0)

    n_steady = n_chunks - 1

    def group(u, carry):
        for j in range(ATTN_UNROLL):
            iteration(ATTN_UNROLL * u + j, j % 2)
        return carry

    n_groups = n_steady // ATTN_UNROLL
    lax.fori_loop(0, n_groups, group, 0)
    for t in range(n_groups * ATTN_UNROLL, n_steady):
        iteration(t, t % 2)
    last = n_chunks - 1
    softmax(last % 2)
    values(max(last - 1, 0), 1 - last % 2)
    values(last, last % 2)

    acc = acc_ref[...]
    o = acc[:, :HEAD_DIM] / acc[:, HEAD_DIM:]
    for g in range(KV_GROUP):
        o_ref[:, g * HEAD_DIM:(g + 1) * HEAD_DIM] = o[g * TM:(g + 1) * TM].astype(o_ref.dtype)


def _key_chunk(T):
    return max(c for c in range(LANES, TK_MAX + 1, LANES) if T % c == 0)


def _attn_bound_kernel(n_chunks, tk, q_ref, k_ref, v_ref, o_ref, lmin_ref, p_ref, m_ref, acc_ref, kmax_ref):
    i = pl.program_id(1)
    rows = KV_GROUP * TM
    ones = jnp.ones((tk, HEAD_DIM), jnp.bfloat16)

    @pl.when(i == 0)
    def _():
        def body(c, mx):
            k = k_ref[0, pl.ds(pl.multiple_of(c * tk, tk), tk), :].astype(jnp.float32)
            n2 = jnp.sum(k * k, axis=1, keepdims=True)
            return jnp.maximum(mx, jnp.max(n2, axis=0, keepdims=True))

        kmax2 = lax.fori_loop(0, n_chunks, body, jnp.zeros((1, 1), jnp.float32))
        kmax_ref[...] = jnp.broadcast_to(kmax2, kmax_ref.shape)

    qf = q_ref[...].reshape(rows, HEAD_DIM).astype(jnp.float32)
    qn2 = jnp.sum(qf * qf, axis=1, keepdims=True)
    m_ref[...] = jnp.broadcast_to(jnp.sqrt(qn2 * kmax_ref[0:1, 0:1]), (rows, LANES))
    acc_ref[...] = jnp.zeros(acc_ref.shape, jnp.float32)

    def probs(c, slot):
        k = k_ref[0, pl.ds(pl.multiple_of(c * tk, tk), tk), :]
        q = q_ref[...].reshape(rows, HEAD_DIM)
        s = lax.dot_general(q, k, (((1,), (1,)), ((), ())), preferred_element_type=jnp.float32)
        p_ref[slot] = jnp.exp2(s - _lane_repeat(m_ref[...], tk // LANES)).astype(jnp.bfloat16)

    def values(c, slot):
        v = v_ref[0, pl.ds(pl.multiple_of(c * tk, tk), tk), :]
        vx = jnp.concatenate([v, ones], axis=1)
        acc_ref[...] += jnp.dot(p_ref[slot], vx, preferred_element_type=jnp.float32)

    def iteration(t, par):
        probs(t + 1, 1 - par)
        values(t, par)

    probs(0, 0)
    n_steady = n_chunks - 1

    def group(u, carry):
        for j in range(ATTN_UNROLL):
            iteration(ATTN_UNROLL * u + j, j % 2)
        return carry

    n_groups = n_steady // ATTN_UNROLL
    lax.fori_loop(0, n_groups, group, 0)
    for t in range(n_groups * ATTN_UNROLL, n_steady):
        iteration(t, t % 2)
    values(n_chunks - 1, (n_chunks - 1) % 2)

    acc = acc_ref[...]
    l = acc[:, HEAD_DIM:]
    o = acc[:, :HEAD_DIM] / l
    for g in range(KV_GROUP):
        o_ref[:, g * HEAD_DIM:(g + 1) * HEAD_DIM] = o[g * TM:(g + 1) * TM].astype(o_ref.dtype)
    lmin_ref[0] = jnp.broadcast_to(jnp.min(l, axis=0, keepdims=True), (SUBLANES, LANES))


def _attention_bound(q, k, v, S):
    T = k.shape[1]
    tk = _key_chunk(T)
    rows = KV_GROUP * TM
    n_q = S // TM
    return pl.pallas_call(
        functools.partial(_attn_bound_kernel, T // tk, tk),
        grid=(N_KV_HEADS, n_q),
        in_specs=[
            pl.BlockSpec((KV_GROUP, TM, HEAD_DIM), lambda g, i: (g, i, 0)),
            pl.BlockSpec((1, T, HEAD_DIM), lambda g, i: (g, 0, 0)),
            pl.BlockSpec((1, T, HEAD_DIM), lambda g, i: (g, 0, 0)),
        ],
        out_specs=(
            pl.BlockSpec((TM, KV_GROUP * HEAD_DIM), lambda g, i: (i, g)),
            pl.BlockSpec((1, SUBLANES, LANES), lambda g, i: (g * n_q + i, 0, 0)),
        ),
        out_shape=(
            jax.ShapeDtypeStruct((S, N_HEADS * HEAD_DIM), jnp.bfloat16),
            jax.ShapeDtypeStruct((N_KV_HEADS * n_q, SUBLANES, LANES), jnp.float32),
        ),
        scratch_shapes=[
            pltpu.VMEM((2, rows, tk), jnp.bfloat16),
            pltpu.VMEM((rows, LANES), jnp.float32),
            pltpu.VMEM((rows, 2 * HEAD_DIM), jnp.float32),
            pltpu.VMEM((SUBLANES, LANES), jnp.float32),
        ],
        compiler_params=pltpu.CompilerParams(dimension_semantics=("arbitrary", "arbitrary"),
                                             vmem_limit_bytes=ATTN_VMEM_LIMIT),
        name="flash_attention_bound",
    )(q, k, v)


def _attention(q, k, v, S):
    o, lmin = _attention_bound(q, k, v, S)
    ok = jnp.min(lmin) >= ATTN_ROW_SUM_FLOOR
    return lax.cond(ok, lambda: o, lambda: _attention_running_max(q, k, v, S))


def _attention_running_max(q, k, v, S):
    T = k.shape[1]
    tk = _key_chunk(T)
    rows = KV_GROUP * TM
    return pl.pallas_call(
        functools.partial(_attn_kernel, T // tk, tk),
        grid=(N_KV_HEADS, S // TM),
        in_specs=[
            pl.BlockSpec((KV_GROUP, TM, HEAD_DIM), lambda g, i: (g, i, 0)),
            pl.BlockSpec((1, T, HEAD_DIM), lambda g, i: (g, 0, 0)),
            pl.BlockSpec((1, T, HEAD_DIM), lambda g, i: (g, 0, 0)),
        ],
        out_specs=pl.BlockSpec((TM, KV_GROUP * HEAD_DIM), lambda g, i: (i, g)),
        out_shape=jax.ShapeDtypeStruct((S, N_HEADS * HEAD_DIM), jnp.bfloat16),
        scratch_shapes=[
            pltpu.VMEM((2, rows, tk), jnp.float32),
            pltpu.VMEM((2, rows, tk), jnp.bfloat16),
            pltpu.VMEM((2, rows, LANES), jnp.float32),
            pltpu.VMEM((2, rows, LANES), jnp.float32),
            pltpu.VMEM((rows, LANES), jnp.float32),
            pltpu.VMEM((rows, 2 * HEAD_DIM), jnp.float32),
        ],
        compiler_params=pltpu.CompilerParams(dimension_semantics=("arbitrary", "arbitrary"),
                                             vmem_limit_bytes=ATTN_VMEM_LIMIT),
        name="flash_attention",
    )(q, k, v)


def _router_out_shapes(S):
    return (
        jax.ShapeDtypeStruct((S, D_MODEL), jnp.float32),
        jax.ShapeDtypeStruct((S * SUBLANES, LANES), jnp.float32),
        jax.ShapeDtypeStruct((S, LANES), jnp.int32),
        jax.ShapeDtypeStruct((S, LANES), jnp.float32),
        jax.ShapeDtypeStruct((SUBLANES, LANES), jnp.int32),
    )


def _router_out_specs():
    return (
        pl.BlockSpec((TM, D_MODEL), lambda i: (i, 0)),
        pl.BlockSpec((TM * SUBLANES, LANES), lambda i: (i, 0)),
        pl.BlockSpec((TM, LANES), lambda i: (i, 0)),
        pl.BlockSpec((TM, LANES), lambda i: (i, 0)),
        pl.BlockSpec((SUBLANES, LANES), lambda i: (0, 0)),
    )


def _router_core(x_new, vec_ref, rw_ref, rb_ref, cnt_ref, x_out_ref, hf_ref, meta_ref, gate_ref,
                 counts_ref):
    i = pl.program_id(0)

    @pl.when(i == 0)
    def _():
        cnt_ref[...] = jnp.zeros_like(cnt_ref)

    x_out_ref[...] = x_new
    hf = _rms(x_new) * vec_ref[3:4, :] * (1.0 + vec_ref[2:3, :]) + vec_ref[1:2, :]
    for c in range(D_CHUNKS):
        hf_ref[pl.ds(c, TM, stride=SUBLANES), :] = hf[:, c * LANES:(c + 1) * LANES]

    hf_hi = hf.astype(jnp.bfloat16)
    hf_lo = (hf - hf_hi.astype(jnp.float32)).astype(jnp.bfloat16)
    cross = (jnp.dot(hf_hi, rw_ref[...], preferred_element_type=jnp.float32)
             + jnp.dot(hf_lo, rw_ref[...], preferred_element_type=jnp.float32))
    logits = cross[:, :LANES] + cross[:, LANES:] + rb_ref[...]
    lane = lax.broadcasted_iota(jnp.int32, logits.shape, 1)
    lane_f = lane.astype(jnp.float32)
    vals, idxs, hots = [], [], []
    work = logits
    for _ in range(TOP_K):
        mx = jnp.max(work, axis=1, keepdims=True)
        idx = jnp.min(jnp.where(work == mx, lane_f, float(LANES)), axis=1, keepdims=True)
        hot = lane_f == idx
        work = jnp.where(hot, -jnp.inf, work)
        vals.append(mx)
        idxs.append(idx.astype(jnp.int32))
        hots.append(hot)
    exps = [jnp.exp(vk - vals[0]) for vk in vals]
    denom = exps[0] + exps[1] + exps[2] + exps[3]

    any_hot = hots[0] | hots[1] | hots[2] | hots[3]
    hot_f = jnp.where(any_hot, 1.0, 0.0)
    r_i = lax.broadcasted_iota(jnp.int32, (TM, TM), 0)
    c_i = lax.broadcasted_iota(jnp.int32, (TM, TM), 1)
    lower = jnp.where(c_i < r_i, 1.0, 0.0).astype(jnp.bfloat16)
    before = jnp.dot(lower, hot_f.astype(jnp.bfloat16), preferred_element_type=jnp.float32) + cnt_ref[0:1, :]
    cnt_ref[0:1, :] = cnt_ref[0:1, :] + jnp.sum(hot_f, axis=0, keepdims=True)

    meta = jnp.zeros(logits.shape, jnp.int32)
    gates = jnp.zeros(logits.shape, jnp.float32)
    for k in range(TOP_K):
        rank = jnp.sum(jnp.where(hots[k], before, 0.0), axis=1, keepdims=True).astype(jnp.int32)
        meta = jnp.where(lane == k, idxs[k], meta)
        meta = jnp.where(lane == TOP_K + k, rank, meta)
        gates = jnp.where(lane == k, exps[k] / denom, gates)
    meta_ref[...] = meta
    gate_ref[...] = gates
    counts_ref[...] = jnp.broadcast_to(cnt_ref[0:1, :], counts_ref.shape).astype(jnp.int32)


def _post_attn_kernel(x_ref, o_ref, wo_ref, vec_ref, rw_ref, rb_ref,
                      x_out_ref, hf_ref, meta_ref, gate_ref, counts_ref, cnt_ref):
    y = jnp.dot(o_ref[...], wo_ref[...], preferred_element_type=jnp.float32)
    x_new = x_ref[...] + vec_ref[0:1, :] * y
    _router_core(x_new, vec_ref, rw_ref, rb_ref, cnt_ref, x_out_ref, hf_ref, meta_ref, gate_ref,
                 counts_ref)


def _post_attn(x, o, w_o, vecs, rw, rb):
    S = x.shape[0]
    return pl.pallas_call(
        _post_attn_kernel,
        grid=(S // TM,),
        in_specs=[
            pl.BlockSpec((TM, D_MODEL), lambda i: (i, 0)),
            pl.BlockSpec((TM, D_MODEL), lambda i: (i, 0)),
            pl.BlockSpec((D_MODEL, D_MODEL), lambda i: (0, 0)),
            pl.BlockSpec((SUBLANES, D_MODEL), lambda i: (0, 0)),
            pl.BlockSpec((D_MODEL, 2 * LANES), lambda i: (0, 0)),
            pl.BlockSpec((1, LANES), lambda i: (0, 0)),
        ],
        out_specs=_router_out_specs(),
        out_shape=_router_out_shapes(S),
        scratch_shapes=[pltpu.VMEM((SUBLANES, LANES), jnp.float32)],
        compiler_params=_cparams(("arbitrary",)),
        name="post_attn_router",
    )(x, o, w_o, vecs, rw, rb)


def _pool_kernel(S, x_ref, prev_ref, next_ref, mvec_ref, pw_ref, vec_ref, rw_ref, rb_ref,
                 x_out_ref, hf_ref, meta_ref, gate_ref, counts_ref, cnt_ref, buf_ref):
    i = pl.program_id(0)
    n = pl.num_programs(0)

    def hmod(xx):
        return _rms(xx) * mvec_ref[2:3, :] * (1.0 + mvec_ref[1:2, :]) + mvec_ref[0:1, :]

    x_cur = x_ref[...]
    h_cur = hmod(x_cur)
    buf_ref[0:POOL_HALO, :] = jnp.where(i > 0, hmod(prev_ref[...]), 0.0)
    buf_ref[POOL_HALO:POOL_HALO + TM, :] = h_cur
    buf_ref[POOL_HALO + TM:, :] = jnp.where(i < n - 1, hmod(next_ref[...]), 0.0)

    t = i * TM + lax.broadcasted_iota(jnp.int32, (TM, 1), 0)
    ys = []
    for g, w in enumerate(POOL_WINDOWS):
        c0 = g * POOL_GROUP
        acc = None
        for d in range(-(w // 2), w - w // 2):
            piece = buf_ref[POOL_HALO + d:POOL_HALO + d + TM, c0:c0 + POOL_GROUP]
            acc = piece if acc is None else acc + piece
        cnt = jnp.minimum(t - w // 2 + w, S) - jnp.maximum(t - w // 2, 0)
        pooled = acc / cnt.astype(jnp.float32) - h_cur[:, c0:c0 + POOL_GROUP]
        ys.append(jnp.dot(pooled.astype(jnp.bfloat16), pw_ref[g], preferred_element_type=jnp.float32))
    y = jnp.concatenate(ys, axis=1) * mvec_ref[3:4, :]
    x_new = x_cur + vec_ref[0:1, :] * y
    _router_core(x_new, vec_ref, rw_ref, rb_ref, cnt_ref, x_out_ref, hf_ref, meta_ref, gate_ref,
                 counts_ref)


def _pool(x, mvecs, pool_w, vecs, rw, rb):
    S = x.shape[0]
    hb = TM // POOL_HALO
    n_halo = S // POOL_HALO
    return pl.pallas_call(
        functools.partial(_pool_kernel, S),
        grid=(S // TM,),
        in_specs=[
            pl.BlockSpec((TM, D_MODEL), lambda i: (i, 0)),
            pl.BlockSpec((POOL_HALO, D_MODEL), lambda i: (jnp.maximum(i * hb - 1, 0), 0)),
            pl.BlockSpec((POOL_HALO, D_MODEL), lambda i: (jnp.minimum((i + 1) * hb, n_halo - 1), 0)),
            pl.BlockSpec((SUBLANES, D_MODEL), lambda i: (0, 0)),
            pl.BlockSpec((len(POOL_WINDOWS), POOL_GROUP, POOL_GROUP), lambda i: (0, 0, 0)),
            pl.BlockSpec((SUBLANES, D_MODEL), lambda i: (0, 0)),
            pl.BlockSpec((D_MODEL, 2 * LANES), lambda i: (0, 0)),
            pl.BlockSpec((1, LANES), lambda i: (0, 0)),
        ],
        out_specs=_router_out_specs(),
        out_shape=_router_out_shapes(S),
        scratch_shapes=[
            pltpu.VMEM((SUBLANES, LANES), jnp.float32),
            pltpu.VMEM((TM + 2 * POOL_HALO, D_MODEL), jnp.float32),
        ],
        compiler_params=_cparams(("arbitrary",)),
        name="pool_router",
    )(x, x, x, mvecs, pool_w, vecs, rw, rb)


def _dispatch_kernel(pos_ref, ztile_ref, hf_ref, xs_ref, stage_ref, zeros_ref, load_sem, row_sem, zero_sem):
    i = pl.program_id(0)
    n = pl.num_programs(0)
    tile_rows = TM * SUBLANES
    pairs = TM * TOP_K

    def load(tile, slot):
        src = pl.multiple_of(tile * tile_rows, tile_rows)
        dst = pl.multiple_of(slot * tile_rows, tile_rows)
        return pltpu.make_async_copy(hf_ref.at[pl.ds(src, tile_rows)],
                                     stage_ref.at[pl.ds(dst, tile_rows)], load_sem.at[slot])

    def rows_wait():
        pltpu.make_async_copy(xs_ref.at[pl.ds(0, pairs * SUBLANES)],
                              xs_ref.at[pl.ds(0, pairs * SUBLANES)], row_sem).wait()

    def zero_copy(e):
        start = pl.multiple_of(jnp.maximum(ztile_ref[e], 0) * tile_rows, tile_rows)
        return pltpu.make_async_copy(zeros_ref, xs_ref.at[pl.ds(start, tile_rows)], zero_sem)

    @pl.when(i == 0)
    def _():
        zeros_ref[...] = jnp.zeros_like(zeros_ref)
        load(0, 0).start()

        def zero_start(e, carry):
            @pl.when(ztile_ref[e] >= 0)
            def _():
                zero_copy(e).start()
            return carry

        def zero_wait(e, carry):
            @pl.when(ztile_ref[e] >= 0)
            def _():
                zero_copy(e).wait()
            return carry

        lax.fori_loop(0, 2 * N_EXPERTS, zero_start, 0)
        lax.fori_loop(0, 2 * N_EXPERTS, zero_wait, 0)

    @pl.when(i > 0)
    def _():
        rows_wait()

    @pl.when(i + 1 < n)
    def _():
        load(i + 1, (i + 1) % 2).start()

    slot = i % 2
    load(i, slot).wait()

    def issue(r, carry):
        src = pl.multiple_of(slot * tile_rows + r * SUBLANES, SUBLANES)
        for k in range(TOP_K):
            dst = pl.multiple_of(pos_ref[(i * TM + r) * TOP_K + k] * SUBLANES, SUBLANES)
            pltpu.make_async_copy(stage_ref.at[pl.ds(src, SUBLANES)],
                                  xs_ref.at[pl.ds(dst, SUBLANES)], row_sem).start(priority=k % 2)
        return carry

    lax.fori_loop(0, TM, issue, 0)

    @pl.when(i == n - 1)
    def _():
        rows_wait()


def _dispatch(pos, ztile, hf_tiles, n_rows):
    S = hf_tiles.shape[0] // SUBLANES
    tile_rows = TM * SUBLANES
    return pl.pallas_call(
        _dispatch_kernel,
        grid_spec=pltpu.PrefetchScalarGridSpec(
            num_scalar_prefetch=2,
            grid=(S // TM,),
            in_specs=[pl.BlockSpec(memory_space=pl.ANY)],
            out_specs=pl.BlockSpec(memory_space=pl.ANY),
            scratch_shapes=[
                pltpu.VMEM((2 * tile_rows, LANES), jnp.float32),
                pltpu.VMEM((tile_rows, LANES), jnp.float32),
                pltpu.SemaphoreType.DMA((2,)),
                pltpu.SemaphoreType.DMA(()),
                pltpu.SemaphoreType.DMA(()),
            ],
        ),
        out_shape=jax.ShapeDtypeStruct((n_rows * SUBLANES, LANES), jnp.float32),
        compiler_params=_cparams(("arbitrary",)),
        name="moe_dispatch",
    )(pos, ztile, hf_tiles)


W_CAST_ROWS = 64


def _expert_kernel(te_ref, nused_ref, xs_ref, wgu_ref, bgu_ref, wd_ref, bd_ref, ys_ref,
                   wgu_bf_ref, wd_bf_ref):
    i = pl.program_id(0)
    used = i < nused_ref[0]
    new_expert = (i == 0) | (te_ref[i] != te_ref[jnp.maximum(i - 1, 0)])

    @pl.when(used & new_expert)
    def _():
        def cast_gu(r, carry):
            rows = pl.ds(pl.multiple_of(r * W_CAST_ROWS, W_CAST_ROWS), W_CAST_ROWS)
            wgu_bf_ref[rows, :] = wgu_ref[rows, :].astype(jnp.bfloat16)
            return carry

        def cast_d(r, carry):
            rows = pl.ds(pl.multiple_of(r * W_CAST_ROWS, W_CAST_ROWS), W_CAST_ROWS)
            wd_bf_ref[rows, :] = wd_ref[rows, :].astype(jnp.bfloat16)
            return carry

        lax.fori_loop(0, D_MODEL // W_CAST_ROWS, cast_gu, 0)
        lax.fori_loop(0, D_FF // W_CAST_ROWS, cast_d, 0)

    @pl.when(used)
    def _():
        cols = [xs_ref[pl.ds(c, TM, stride=SUBLANES), :] for c in range(D_CHUNKS)]
        x = jnp.concatenate(cols, axis=1).astype(jnp.bfloat16)
        gu = jnp.dot(x, wgu_bf_ref[...], preferred_element_type=jnp.float32) + bgu_ref[...]
        gate = jnp.minimum(gu[:, :D_FF], SWIGLU_LIMIT)
        up = jnp.clip(gu[:, D_FF:], -SWIGLU_LIMIT, SWIGLU_LIMIT)
        glu = gate / (1.0 + jnp.exp(-SWIGLU_ALPHA * gate))
        a = ((up + 1.0) * glu).astype(jnp.bfloat16)
        y = jnp.dot(a, wd_bf_ref[...], preferred_element_type=jnp.float32) + bd_ref[...]
        for c in range(D_CHUNKS):
            ys_ref[pl.ds(c, TM, stride=SUBLANES), :] = y[:, c * LANES:(c + 1) * LANES]

    @pl.when(jnp.logical_not(used))
    def _():
        ys_ref[...] = jnp.zeros_like(ys_ref)


def _experts(layer, tile_expert, n_used, xs, w_gu, b_gu, w_down, b_down):
    n_tiles = tile_expert.shape[0]
    L, E = w_gu.shape[:2]

    def row_map(i, te, nu):
        return (jnp.minimum(i, nu[0] - 1), 0)

    def w_map(i, te, nu):
        return (layer, te[i], 0, 0)

    return pl.pallas_call(
        _expert_kernel,
        grid_spec=pltpu.PrefetchScalarGridSpec(
            num_scalar_prefetch=2,
            grid=(n_tiles,),
            in_specs=[
                pl.BlockSpec((TM * SUBLANES, LANES), row_map),
                pl.BlockSpec((None, None, D_MODEL, 2 * D_FF), w_map),
                pl.BlockSpec((None, None, 1, 2 * D_FF), w_map),
                pl.BlockSpec((None, None, D_FF, D_MODEL), w_map),
                pl.BlockSpec((None, None, 1, D_MODEL), w_map),
            ],
            out_specs=pl.BlockSpec((TM * SUBLANES, LANES), lambda i, te, nu: (i, 0)),
            scratch_shapes=[
                pltpu.VMEM((D_MODEL, 2 * D_FF), jnp.bfloat16),
                pltpu.VMEM((D_FF, D_MODEL), jnp.bfloat16),
            ],
        ),
        out_shape=jax.ShapeDtypeStruct(xs.shape, jnp.float32),
        compiler_params=_cparams(("arbitrary",)),
        name="moe_experts",
    )(tile_expert, n_used, xs, w_gu, b_gu.reshape(L, E, 1, 2 * D_FF), w_down,
      b_down.reshape(L, E, 1, D_MODEL))


def _combine_kernel(final, pos_ref, ys_ref, x_ref, gate_ref, vec_ref, o_ref, buf_ref, sem):
    i = pl.program_id(0)
    n = pl.num_programs(0)
    slot_rows = TOP_K * TM * SUBLANES

    def issue(tile, slot):
        def one(r, carry):
            for k in range(TOP_K):
                src = pl.multiple_of(pos_ref[(tile * TM + r) * TOP_K + k] * SUBLANES, SUBLANES)
                dst = pl.multiple_of(slot * slot_rows + (k * TM + r) * SUBLANES, SUBLANES)
                pltpu.make_async_copy(ys_ref.at[pl.ds(src, SUBLANES)],
                                      buf_ref.at[pl.ds(dst, SUBLANES)], sem.at[slot]).start(priority=k % 2)
            return carry

        lax.fori_loop(0, TM, one, 0)

    @pl.when(i == 0)
    def _():
        issue(0, 0)

    @pl.when(i + 1 < n)
    def _():
        issue(i + 1, (i + 1) % 2)

    slot = i % 2
    base = pl.multiple_of(slot * slot_rows, slot_rows)
    pltpu.make_async_copy(ys_ref.at[pl.ds(0, slot_rows)], buf_ref.at[pl.ds(base, slot_rows)],
                          sem.at[slot]).wait()

    gates = gate_ref[...]
    pieces = []
    for c in range(D_CHUNKS):
        f = None
        for k in range(TOP_K):
            rows = buf_ref[pl.ds(base + k * TM * SUBLANES + c, TM, stride=SUBLANES), :]
            term = gates[:, k:k + 1] * rows
            f = term if f is None else f + term
        pieces.append(f)
    f = jnp.concatenate(pieces, axis=1)
    x_new = x_ref[...] + vec_ref[0:1, :] * f
    if final:
        x_new = _rms(x_new) * vec_ref[1:2, :]
    o_ref[...] = x_new


def _combine(pos, ys, x, gates, vecs, final):
    S = x.shape[0]
    return pl.pallas_call(
        functools.partial(_combine_kernel, final),
        grid_spec=pltpu.PrefetchScalarGridSpec(
            num_scalar_prefetch=1,
            grid=(S // TM,),
            in_specs=[
                pl.BlockSpec(memory_space=pl.ANY),
                pl.BlockSpec((TM, D_MODEL), lambda i, p: (i, 0)),
                pl.BlockSpec((TM, LANES), lambda i, p: (i, 0)),
                pl.BlockSpec((SUBLANES, D_MODEL), lambda i, p: (0, 0)),
            ],
            out_specs=pl.BlockSpec((TM, D_MODEL), lambda i, p: (i, 0)),
            scratch_shapes=[
                pltpu.VMEM((2 * TOP_K * TM * SUBLANES, LANES), jnp.float32),
                pltpu.SemaphoreType.DMA((2,)),
            ],
        ),
        out_shape=jax.ShapeDtypeStruct((S, D_MODEL), jnp.float32),
        compiler_params=_cparams(("arbitrary",)),
        name="moe_combine",
    )(pos, ys, x, gates, vecs)


def _moe(layer, x_new, hf_tiles, meta, gates, counts, w_gu, b_gu, w_down, b_down, out_vecs, final):
    S = x_new.shape[0]
    n_tiles = (S * TOP_K) // TM + N_EXPERTS
    cnt = counts[0, :N_EXPERTS]
    padded = ((cnt + TM - 1) // TM) * TM
    ends = jnp.cumsum(padded)
    offs = ends - padded
    eid = meta[:, :TOP_K]
    rank = meta[:, TOP_K:2 * TOP_K]
    pos = (offs[eid] + rank).reshape(-1).astype(jnp.int32)
    tile_ends = ends // TM
    tile_ids = jnp.arange(n_tiles, dtype=jnp.int32)
    tile_expert = jnp.minimum(
        jnp.sum((tile_ends[None, :] <= tile_ids[:, None]).astype(jnp.int32), axis=1),
        N_EXPERTS - 1).astype(jnp.int32)
    n_used = tile_ends[-1:].astype(jnp.int32)
    trailing = tile_ends[-1] + jnp.arange(N_EXPERTS, dtype=jnp.int32)
    ztile = jnp.concatenate([
        jnp.where(padded > 0, tile_ends - 1, -1),
        jnp.where(trailing < n_tiles, trailing, -1)]).astype(jnp.int32)
    xs = _dispatch(pos, ztile, hf_tiles, n_tiles * TM)
    ys = _experts(layer, tile_expert, n_used, xs, w_gu, b_gu, w_down, b_down)
    return _combine(pos, ys, x_new, gates, out_vecs, final)


def _pad_rows(rows):
    rows = [r.reshape(1, -1) for r in rows]
    rows += [jnp.zeros_like(rows[0])] * (SUBLANES - len(rows))
    return jnp.concatenate(rows, axis=0)


def _rope_tables(S, C):
    quarter = HEAD_DIM // 4
    inv_freq = ROPE_THETA ** (-np.arange(quarter, dtype=np.float32) / quarter)
    n_rows = S // GRID_W
    ang_r = np.arange(n_rows, dtype=np.float32)[:, None] * inv_freq[None, :]
    ang_c = np.arange(GRID_W, dtype=np.float32)[:, None] * inv_freq[None, :]

    def expand(tab_r, tab_c, sign):
        r = jnp.repeat(jnp.asarray(tab_r, jnp.float32), GRID_W, axis=0)
        cc = jnp.tile(jnp.asarray(tab_c, jnp.float32), (n_rows, 1))
        return jnp.concatenate([sign * r, r, sign * cc, cc], axis=1)

    cos = expand(np.cos(ang_r), np.cos(ang_c), 1.0)
    sin = expand(np.sin(ang_r), np.sin(ang_c), -1.0)
    cos = jnp.concatenate([cos, jnp.ones((C, HEAD_DIM), jnp.float32)], axis=0)
    sin = jnp.concatenate([sin, jnp.zeros((C, HEAD_DIM), jnp.float32)], axis=0)
    return cos, sin


def kernel(x, c, ctx, c_ctx, ada_w, ada_b, norm_mix, norm_ffn, attn_w_qkv, attn_q_norm, attn_k_norm,
           attn_w_o, pool_w, pool_scale, moe_router_w, moe_router_b, moe_w_gu, moe_b_gu, moe_w_down,
           moe_b_down, final_norm):
    B, S, D = x.shape
    C = ctx.shape[1]
    assert B == 1 and D == D_MODEL and S % TM == 0 and C % TM == 0 and S % GRID_W == 0
    x2d = x.reshape(S, D)
    ctx2d = ctx.reshape(C, D)
    bf = jnp.bfloat16

    cvec = _pad_rows([c.reshape(-1), c_ctx])
    mod = _modulation(cvec, ada_w, ada_b)
    m_l = [mod[l, 0].reshape(N_MOD, D) for l in range(2)]
    m_c0 = mod[0, 1].reshape(N_MOD, D)

    def split_router(w):
        w = jnp.pad(w, ((0, 0), (0, LANES - N_EXPERTS)))
        hi = w.astype(bf)
        lo = (w - hi.astype(jnp.float32)).astype(bf)
        return jnp.concatenate([hi, lo], axis=1)

    rw = [split_router(moe_router_w[l]) for l in range(2)]
    rb = [jnp.pad(moe_router_b[l], (0, LANES - N_EXPERTS), constant_values=NEG_BIG).reshape(1, LANES)
          for l in range(2)]

    cos_t, sin_t = _rope_tables(S, C)
    qkv_vecs = _pad_rows([m_l[0][0], m_l[0][1], m_c0[0], m_c0[1], norm_mix[0]])
    q, k, v = _qkv(x2d, ctx2d, qkv_vecs, attn_w_qkv[0].astype(bf), attn_q_norm[0].reshape(1, -1),
                   attn_k_norm[0].reshape(1, -1), cos_t, sin_t)
    o = _attention(q, k, v, S)
    vecs0 = _pad_rows([m_l[0][2], m_l[0][3], m_l[0][4], norm_ffn[0]])
    x1, hf, meta, gates, counts = _post_attn(x2d, o, attn_w_o[0].astype(bf), vecs0, rw[0], rb[0])
    x2 = _moe(0, x1, hf, meta, gates, counts, moe_w_gu, moe_b_gu, moe_w_down, moe_b_down,
              _pad_rows([m_l[0][5]]), final=False)

    mvecs = _pad_rows([m_l[1][0], m_l[1][1], norm_mix[1], pool_scale[0]])
    vecs1 = _pad_rows([m_l[1][2], m_l[1][3], m_l[1][4], norm_ffn[1]])
    x3, hf, meta, gates, counts = _pool(x2, mvecs, pool_w[0].astype(bf), vecs1, rw[1], rb[1])
    out = _moe(1, x3, hf, meta, gates, counts, moe_w_gu, moe_b_gu, moe_w_down, moe_b_down,
               _pad_rows([m_l[1][5], final_norm]), final=True)
    return out.reshape(B, S, D)
```

```python
import functools

import numpy as np
import jax
import jax.numpy as jnp
from jax import lax
from jax.experimental import pallas as pl
from jax.experimental.pallas import tpu as pltpu

D_MODEL = 1024
GRID_W = 64
N_HEADS = 8
N_KV_HEADS = 2
KV_GROUP = N_HEADS // N_KV_HEADS
HEAD_DIM = 128
ROPE_THETA = 10000.0
POOL_WINDOWS = (2, 4, 8, 16)
POOL_GROUP = D_MODEL // len(POOL_WINDOWS)
POOL_HALO = 8
N_EXPERTS = 32
TOP_K = 4
D_FF = D_MODEL
SWIGLU_LIMIT = 7.0
SWIGLU_ALPHA = 1.702
NORM_EPS = 1e-6
N_MOD = 6

LANES = 128
SUBLANES = 8
D_CHUNKS = D_MODEL // LANES
TM = 256
EM = 512
TK_MAX = 1280
ATTN_ROW_SUM_FLOOR = 2.0 ** -100
ATTN_UNROLL = 2
VMEM_LIMIT = 48 * 1024 * 1024
ATTN_VMEM_LIMIT = 56 * 1024 * 1024
LOG2E = 1.4426950408889634
NEG_BIG = -1e30

_HI = lax.Precision.HIGHEST


def _cparams(sem):
    return pltpu.CompilerParams(dimension_semantics=sem, vmem_limit_bytes=VMEM_LIMIT)


def _rms(x, eps=NORM_EPS):
    return x * lax.rsqrt(jnp.mean(x * x, axis=-1, keepdims=True) + eps)


def _mod_kernel(cv_ref, w_ref, b_ref, o_ref):
    cv = cv_ref[...]
    s = cv / (1.0 + jnp.exp(-cv))
    o_ref[0] = jnp.dot(s, w_ref[0], precision=_HI, preferred_element_type=jnp.float32) + b_ref[0]


def _modulation(cvec, ada_w, ada_b):
    depth = ada_w.shape[0]
    nblk = 4
    bw = N_MOD * D_MODEL // nblk
    return pl.pallas_call(
        _mod_kernel,
        grid=(depth, nblk),
        in_specs=[
            pl.BlockSpec((SUBLANES, D_MODEL), lambda l, j: (0, 0)),
            pl.BlockSpec((1, D_MODEL, bw), lambda l, j: (l, 0, j)),
            pl.BlockSpec((1, 1, bw), lambda l, j: (l, 0, j)),
        ],
        out_specs=pl.BlockSpec((1, SUBLANES, bw), lambda l, j: (l, 0, j)),
        out_shape=jax.ShapeDtypeStruct((depth, SUBLANES, N_MOD * D_MODEL), jnp.float32),
        compiler_params=_cparams(("arbitrary", "arbitrary")),
        name="modulation",
    )(cvec, ada_w, ada_b.reshape(depth, 1, N_MOD * D_MODEL))


def _rope(x, cos, sin_signed):
    lane = lax.broadcasted_iota(jnp.int32, x.shape, 1)
    partner = jnp.where((lane % 64) < 32, pltpu.roll(x, LANES - 32, 1), pltpu.roll(x, 32, 1))
    return x * cos + partner * sin_signed


def _qkv_kernel(n_lat_tiles, x_ref, ctx_ref, vec_ref, w_ref, qg_ref, kg_ref, cos_ref, sin_ref,
                q_ref, k_ref, v_ref):
    i = pl.program_id(0)
    is_ctx = i >= n_lat_tiles
    xt = jnp.where(is_ctx, ctx_ref[...], x_ref[...])
    sh = jnp.where(is_ctx, vec_ref[2:3, :], vec_ref[0:1, :])
    sc = jnp.where(is_ctx, vec_ref[3:4, :], vec_ref[1:2, :])
    h = _rms(xt) * vec_ref[4:5, :] * (1.0 + sc) + sh
    qkv = jnp.dot(h.astype(jnp.bfloat16), w_ref[...], preferred_element_type=jnp.float32)
    cos = cos_ref[...]
    sin = sin_ref[...]
    q_scale = (HEAD_DIM ** -0.5) * LOG2E
    for hd in range(N_HEADS):
        qh = _rms(qkv[:, hd * HEAD_DIM:(hd + 1) * HEAD_DIM]) * qg_ref[...]
        q_ref[hd] = (_rope(qh, cos, sin) * q_scale).astype(jnp.bfloat16)
    for kv in range(N_KV_HEADS):
        c0 = (N_HEADS + kv) * HEAD_DIM
        kh = _rms(qkv[:, c0:c0 + HEAD_DIM]) * kg_ref[...]
        k_ref[kv] = _rope(kh, cos, sin).astype(jnp.bfloat16)
        c1 = (N_HEADS + N_KV_HEADS + kv) * HEAD_DIM
        v_ref[kv] = qkv[:, c1:c1 + HEAD_DIM].astype(jnp.bfloat16)


def _qkv(x, ctx, vecs, w_qkv, q_g, k_g, cos_t, sin_t):
    S, C = x.shape[0], ctx.shape[0]
    n_lat = S // TM
    n_ctx = C // TM
    T = S + C
    last = n_lat - 1
    qkv_w = w_qkv.shape[1]
    out_shapes = (
        jax.ShapeDtypeStruct((N_HEADS, T, HEAD_DIM), jnp.bfloat16),
        jax.ShapeDtypeStruct((N_KV_HEADS, T, HEAD_DIM), jnp.bfloat16),
        jax.ShapeDtypeStruct((N_KV_HEADS, T, HEAD_DIM), jnp.bfloat16),
    )
    return pl.pallas_call(
        functools.partial(_qkv_kernel, n_lat),
        grid=(n_lat + n_ctx,),
        in_specs=[
            pl.BlockSpec((TM, D_MODEL), lambda i: (jnp.minimum(i, last), 0)),
            pl.BlockSpec((TM, D_MODEL), lambda i: (jnp.maximum(i - n_lat, 0), 0)),
            pl.BlockSpec((SUBLANES, D_MODEL), lambda i: (0, 0)),
            pl.BlockSpec((D_MODEL, qkv_w), lambda i: (0, 0)),
            pl.BlockSpec((1, HEAD_DIM), lambda i: (0, 0)),
            pl.BlockSpec((1, HEAD_DIM), lambda i: (0, 0)),
            pl.BlockSpec((TM, HEAD_DIM), lambda i: (i, 0)),
            pl.BlockSpec((TM, HEAD_DIM), lambda i: (i, 0)),
        ],
        out_specs=(
            pl.BlockSpec((N_HEADS, TM, HEAD_DIM), lambda i: (0, i, 0)),
            pl.BlockSpec((N_KV_HEADS, TM, HEAD_DIM), lambda i: (0, i, 0)),
            pl.BlockSpec((N_KV_HEADS, TM, HEAD_DIM), lambda i: (0, i, 0)),
        ),
        out_shape=out_shapes,
        compiler_params=_cparams(("arbitrary",)),
        name="qkv_proj",
    )(x, ctx, vecs, w_qkv, q_g, k_g, cos_t, sin_t)


def _lane_repeat(x, n):
    return jnp.concatenate([x] * n, axis=1)


def _attn_kernel(n_chunks, tk, q_ref, k_ref, v_ref, o_ref, s_ref, p_ref, a_ref, mc_ref, m_ref, acc_ref):
    rows = KV_GROUP * TM
    ones = jnp.ones((tk, HEAD_DIM), jnp.bfloat16)

    def scores(c, slot):
        k = k_ref[0, pl.ds(pl.multiple_of(c * tk, tk), tk), :]
        q = q_ref[...].reshape(rows, HEAD_DIM)
        s = lax.dot_general(q, k, (((1,), (1,)), ((), ())), preferred_element_type=jnp.float32)
        s_ref[slot] = s
        mc_ref[slot] = jnp.broadcast_to(jnp.max(s, axis=1, keepdims=True), (rows, LANES))

    def softmax(slot):
        m_old = m_ref[...]
        m_new = jnp.maximum(m_old, mc_ref[slot])
        a_ref[slot] = jnp.exp2(m_old - m_new)
        m_ref[...] = m_new
        p_ref[slot] = jnp.exp2(s_ref[slot] - _lane_repeat(m_new, tk // LANES)).astype(jnp.bfloat16)

    def values(c, slot):
        v = v_ref[0, pl.ds(pl.multiple_of(c * tk, tk), tk), :]
        vx = jnp.concatenate([v, ones], axis=1)
        acc_ref[...] = (acc_ref[...] * _lane_repeat(a_ref[slot], 2)
                        + jnp.dot(p_ref[slot], vx, preferred_element_type=jnp.float32))

    def iteration(t, par):
        scores(t + 1, 1 - par)
        softmax(par)
        values(jnp.maximum(t - 1, 0), 1 - par)

    m_ref[...] = jnp.full(m_ref.shape, -jnp.inf, jnp.float32)
    acc_ref[...] = jnp.zeros(acc_ref.shape, jnp.float32)
    p_ref[1] = jnp.zeros(p_ref.shape[1:], jnp.bfloat16)
    a_ref[1] = jnp.ones(a_ref.shape[1:], jnp.float32)
    scores(0, 0)

    n_steady = n_chunks - 1

    def group(u, carry):
        for j in range(ATTN_UNROLL):
            iteration(ATTN_UNROLL * u + j, j % 2)
        return carry

    n_groups = n_steady // ATTN_UNROLL
    lax.fori_loop(0, n_groups, group, 0)
    for t in range(n_groups * ATTN_UNROLL, n_steady):
        iteration(t, t % 2)
    last = n_chunks - 1
    softmax(last % 2)
    values(max(last - 1, 0), 1 - last % 2)
    values(last, last % 2)

    acc = acc_ref[...]
    o = acc[:, :HEAD_DIM] / acc[:, HEAD_DIM:]
    for g in range(KV_GROUP):
        o_ref[:, g * HEAD_DIM:(g + 1) * HEAD_DIM] = o[g * TM:(g + 1) * TM].astype(o_ref.dtype)


def _key_chunk(T):
    return max(c for c in range(LANES, TK_MAX + 1, LANES) if T % c == 0)


def _attn_bound_kernel(n_chunks, tk, q_ref, k_ref, v_ref, o_ref, lmin_ref, p_ref, m_ref, acc_ref, kmax_ref):
    i = pl.program_id(1)
    rows = KV_GROUP * TM
    ones = jnp.ones((tk, HEAD_DIM), jnp.bfloat16)

    @pl.when(i == 0)
    def _():
        def body(c, mx):
            k = k_ref[0, pl.ds(pl.multiple_of(c * tk, tk), tk), :].astype(jnp.float32)
            n2 = jnp.sum(k * k, axis=1, keepdims=True)
            return jnp.maximum(mx, jnp.max(n2, axis=0, keepdims=True))

        kmax2 = lax.fori_loop(0, n_chunks, body, jnp.zeros((1, 1), jnp.float32))
        kmax_ref[...] = jnp.broadcast_to(kmax2, kmax_ref.shape)

    qf = q_ref[...].reshape(rows, HEAD_DIM).astype(jnp.float32)
    qn2 = jnp.sum(qf * qf, axis=1, keepdims=True)
    m_ref[...] = jnp.broadcast_to(jnp.sqrt(qn2 * kmax_ref[0:1, 0:1]), (rows, LANES))
    acc_ref[...] = jnp.zeros(acc_ref.shape, jnp.float32)

    def probs(c, slot):
        k = k_ref[0, pl.ds(pl.multiple_of(c * tk, tk), tk), :]
        q = q_ref[...].reshape(rows, HEAD_DIM)
        s = lax.dot_general(q, k, (((1,), (1,)), ((), ())), preferred_element_type=jnp.float32)
        p_ref[slot] = jnp.exp2(s - _lane_repeat(m_ref[...], tk // LANES)).astype(jnp.bfloat16)

    def values(c, slot):
        v = v_ref[0, pl.ds(pl.multiple_of(c * tk, tk), tk), :]
        vx = jnp.concatenate([v, ones], axis=1)
        acc_ref[...] += jnp.dot(p_ref[slot], vx, preferred_element_type=jnp.float32)

    def iteration(t, par):
        probs(t + 1, 1 - par)
        values(t, par)

    probs(0, 0)
    n_steady = n_chunks - 1

    def group(u, carry):
        for j in range(ATTN_UNROLL):
            iteration(ATTN_UNROLL * u + j, j % 2)
        return carry

    n_groups = n_steady // ATTN_UNROLL
    lax.fori_loop(0, n_groups, group, 0)
    for t in range(n_groups * ATTN_UNROLL, n_steady):
        iteration(t, t % 2)
    values(n_chunks - 1, (n_chunks - 1) % 2)

    acc = acc_ref[...]
    l = acc[:, HEAD_DIM:]
    o = acc[:, :HEAD_DIM] / l
    for g in range(KV_GROUP):
        o_ref[:, g * HEAD_DIM:(g + 1) * HEAD_DIM] = o[g * TM:(g + 1) * TM].astype(o_ref.dtype)
    lmin_ref[0] = jnp.broadcast_to(jnp.min(l, axis=0, keepdims=True), (SUBLANES, LANES))


def _attention_bound(q, k, v, S):
    T = k.shape[1]
    tk = _key_chunk(T)
    rows = KV_GROUP * TM
    n_q = S // TM
    return pl.pallas_call(
        functools.partial(_attn_bound_kernel, T // tk, tk),
        grid=(N_KV_HEADS, n_q),
        in_specs=[
            pl.BlockSpec((KV_GROUP, TM, HEAD_DIM), lambda g, i: (g, i, 0)),
            pl.BlockSpec((1, T, HEAD_DIM), lambda g, i: (g, 0, 0)),
            pl.BlockSpec((1, T, HEAD_DIM), lambda g, i: (g, 0, 0)),
        ],
        out_specs=(
            pl.BlockSpec((TM, KV_GROUP * HEAD_DIM), lambda g, i: (i, g)),
            pl.BlockSpec((1, SUBLANES, LANES), lambda g, i: (g * n_q + i, 0, 0)),
        ),
        out_shape=(
            jax.ShapeDtypeStruct((S, N_HEADS * HEAD_DIM), jnp.bfloat16),
            jax.ShapeDtypeStruct((N_KV_HEADS * n_q, SUBLANES, LANES), jnp.float32),
        ),
        scratch_shapes=[
            pltpu.VMEM((2, rows, tk), jnp.bfloat16),
            pltpu.VMEM((rows, LANES), jnp.float32),
            pltpu.VMEM((rows, 2 * HEAD_DIM), jnp.float32),
            pltpu.VMEM((SUBLANES, LANES), jnp.float32),
        ],
        compiler_params=pltpu.CompilerParams(dimension_semantics=("arbitrary", "arbitrary"),
                                             vmem_limit_bytes=ATTN_VMEM_LIMIT),
        name="flash_attention_bound",
    )(q, k, v)


def _attention(q, k, v, S):
    o, lmin = _attention_bound(q, k, v, S)
    ok = jnp.min(lmin) >= ATTN_ROW_SUM_FLOOR
    return lax.cond(ok, lambda: o, lambda: _attention_running_max(q, k, v, S))


def _attention_running_max(q, k, v, S):
    T = k.shape[1]
    tk = _key_chunk(T)
    rows = KV_GROUP * TM
    return pl.pallas_call(
        functools.partial(_attn_kernel, T // tk, tk),
        grid=(N_KV_HEADS, S // TM),
        in_specs=[
            pl.BlockSpec((KV_GROUP, TM, HEAD_DIM), lambda g, i: (g, i, 0)),
            pl.BlockSpec((1, T, HEAD_DIM), lambda g, i: (g, 0, 0)),
            pl.BlockSpec((1, T, HEAD_DIM), lambda g, i: (g, 0, 0)),
        ],
        out_specs=pl.BlockSpec((TM, KV_GROUP * HEAD_DIM), lambda g, i: (i, g)),
        out_shape=jax.ShapeDtypeStruct((S, N_HEADS * HEAD_DIM), jnp.bfloat16),
        scratch_shapes=[
            pltpu.VMEM((2, rows, tk), jnp.float32),
            pltpu.VMEM((2, rows, tk), jnp.bfloat16),
            pltpu.VMEM((2, rows, LANES), jnp.float32),
            pltpu.VMEM((2, rows, LANES), jnp.float32),
            pltpu.VMEM((rows, LANES), jnp.float32),
            pltpu.VMEM((rows, 2 * HEAD_DIM), jnp.float32),
        ],
        compiler_params=pltpu.CompilerParams(dimension_semantics=("arbitrary", "arbitrary"),
                                             vmem_limit_bytes=ATTN_VMEM_LIMIT),
        name="flash_attention",
    )(q, k, v)


def _router_out_shapes(S):
    return (
        jax.ShapeDtypeStruct((S, D_MODEL), jnp.float32),
        jax.ShapeDtypeStruct((S * SUBLANES, LANES), jnp.float32),
        jax.ShapeDtypeStruct((SUBLANES, S), jnp.int32),
        jax.ShapeDtypeStruct((S, LANES), jnp.float32),
        jax.ShapeDtypeStruct((SUBLANES, LANES), jnp.int32),
    )


def _router_out_specs():
    return (
        pl.BlockSpec((TM, D_MODEL), lambda i: (i, 0)),
        pl.BlockSpec((TM * SUBLANES, LANES), lambda i: (i, 0)),
        pl.BlockSpec((SUBLANES, TM), lambda i: (0, i)),
        pl.BlockSpec((TM, LANES), lambda i: (i, 0)),
        pl.BlockSpec((SUBLANES, LANES), lambda i: (0, 0)),
    )


def _router_core(x_new, vec_ref, rw_ref, rb_ref, cnt_ref, x_out_ref, hf_ref, meta_ref, gate_ref,
                 counts_ref):
    i = pl.program_id(0)

    @pl.when(i == 0)
    def _():
        cnt_ref[...] = jnp.zeros_like(cnt_ref)

    x_out_ref[...] = x_new
    hf = _rms(x_new) * vec_ref[3:4, :] * (1.0 + vec_ref[2:3, :]) + vec_ref[1:2, :]
    for c in range(D_CHUNKS):
        hf_ref[pl.ds(c, TM, stride=SUBLANES), :] = hf[:, c * LANES:(c + 1) * LANES]

    hf_hi = hf.astype(jnp.bfloat16)
    hf_lo = (hf - hf_hi.astype(jnp.float32)).astype(jnp.bfloat16)
    cross = (jnp.dot(hf_hi, rw_ref[...], preferred_element_type=jnp.float32)
             + jnp.dot(hf_lo, rw_ref[...], preferred_element_type=jnp.float32))
    logits = cross[:, :LANES] + cross[:, LANES:] + rb_ref[...]
    lane = lax.broadcasted_iota(jnp.int32, logits.shape, 1)
    lane_f = lane.astype(jnp.float32)
    vals, idxs, hots = [], [], []
    work = logits
    for _ in range(TOP_K):
        mx = jnp.max(work, axis=1, keepdims=True)
        idx = jnp.min(jnp.where(work == mx, lane_f, float(LANES)), axis=1, keepdims=True)
        hot = lane_f == idx
        work = jnp.where(hot, -jnp.inf, work)
        vals.append(mx)
        idxs.append(idx.astype(jnp.int32))
        hots.append(hot)
    exps = [jnp.exp(vk - vals[0]) for vk in vals]
    denom = exps[0] + exps[1] + exps[2] + exps[3]

    any_hot = hots[0] | hots[1] | hots[2] | hots[3]
    hot_f = jnp.where(any_hot, 1.0, 0.0)
    r_i = lax.broadcasted_iota(jnp.int32, (TM, TM), 0)
    c_i = lax.broadcasted_iota(jnp.int32, (TM, TM), 1)
    lower = jnp.where(c_i < r_i, 1.0, 0.0).astype(jnp.bfloat16)
    before = jnp.dot(lower, hot_f.astype(jnp.bfloat16), preferred_element_type=jnp.float32) + cnt_ref[0:1, :]
    cnt_ref[0:1, :] = cnt_ref[0:1, :] + jnp.sum(hot_f, axis=0, keepdims=True)

    meta = jnp.zeros(logits.shape, jnp.float32)
    gates = jnp.zeros(logits.shape, jnp.float32)
    for k in range(TOP_K):
        rank = jnp.sum(jnp.where(hots[k], before, 0.0), axis=1, keepdims=True)
        meta = jnp.where(lane == k, idxs[k].astype(jnp.float32), meta)
        meta = jnp.where(lane == TOP_K + k, rank, meta)
        gates = jnp.where(lane == k, exps[k] / denom, gates)
    meta_ref[...] = meta.T[:SUBLANES, :].astype(jnp.int32)
    gate_ref[...] = gates
    counts_ref[...] = jnp.broadcast_to(cnt_ref[0:1, :], counts_ref.shape).astype(jnp.int32)


def _post_attn_kernel(x_ref, o_ref, wo_ref, vec_ref, rw_ref, rb_ref,
                      x_out_ref, hf_ref, meta_ref, gate_ref, counts_ref, cnt_ref):
    y = jnp.dot(o_ref[...], wo_ref[...], preferred_element_type=jnp.float32)
    x_new = x_ref[...] + vec_ref[0:1, :] * y
    _router_core(x_new, vec_ref, rw_ref, rb_ref, cnt_ref, x_out_ref, hf_ref, meta_ref, gate_ref,
                 counts_ref)


def _post_attn(x, o, w_o, vecs, rw, rb):
    S = x.shape[0]
    return pl.pallas_call(
        _post_attn_kernel,
        grid=(S // TM,),
        in_specs=[
            pl.BlockSpec((TM, D_MODEL), lambda i: (i, 0)),
            pl.BlockSpec((TM, D_MODEL), lambda i: (i, 0)),
            pl.BlockSpec((D_MODEL, D_MODEL), lambda i: (0, 0)),
            pl.BlockSpec((SUBLANES, D_MODEL), lambda i: (0, 0)),
            pl.BlockSpec((D_MODEL, 2 * LANES), lambda i: (0, 0)),
            pl.BlockSpec((1, LANES), lambda i: (0, 0)),
        ],
        out_specs=_router_out_specs(),
        out_shape=_router_out_shapes(S),
        scratch_shapes=[pltpu.VMEM((SUBLANES, LANES), jnp.float32)],
        compiler_params=_cparams(("arbitrary",)),
        name="post_attn_router",
    )(x, o, w_o, vecs, rw, rb)


def _pool_kernel(S, x_ref, prev_ref, next_ref, mvec_ref, pw_ref, vec_ref, rw_ref, rb_ref,
                 x_out_ref, hf_ref, meta_ref, gate_ref, counts_ref, cnt_ref, buf_ref):
    i = pl.program_id(0)
    n = pl.num_programs(0)

    def hmod(xx):
        return _rms(xx) * mvec_ref[2:3, :] * (1.0 + mvec_ref[1:2, :]) + mvec_ref[0:1, :]

    x_cur = x_ref[...]
    h_cur = hmod(x_cur)
    buf_ref[0:POOL_HALO, :] = jnp.where(i > 0, hmod(prev_ref[...]), 0.0)
    buf_ref[POOL_HALO:POOL_HALO + TM, :] = h_cur
    buf_ref[POOL_HALO + TM:, :] = jnp.where(i < n - 1, hmod(next_ref[...]), 0.0)

    t = i * TM + lax.broadcasted_iota(jnp.int32, (TM, 1), 0)
    ys = []
    for g, w in enumerate(POOL_WINDOWS):
        c0 = g * POOL_GROUP
        acc = None
        for d in range(-(w // 2), w - w // 2):
            piece = buf_ref[POOL_HALO + d:POOL_HALO + d + TM, c0:c0 + POOL_GROUP]
            acc = piece if acc is None else acc + piece
        cnt = jnp.minimum(t - w // 2 + w, S) - jnp.maximum(t - w // 2, 0)
        pooled = acc / cnt.astype(jnp.float32) - h_cur[:, c0:c0 + POOL_GROUP]
        ys.append(jnp.dot(pooled.astype(jnp.bfloat16), pw_ref[g], preferred_element_type=jnp.float32))
    y = jnp.concatenate(ys, axis=1) * mvec_ref[3:4, :]
    x_new = x_cur + vec_ref[0:1, :] * y
    _router_core(x_new, vec_ref, rw_ref, rb_ref, cnt_ref, x_out_ref, hf_ref, meta_ref, gate_ref,
                 counts_ref)


def _pool(x, mvecs, pool_w, vecs, rw, rb):
    S = x.shape[0]
    hb = TM // POOL_HALO
    n_halo = S // POOL_HALO
    return pl.pallas_call(
        functools.partial(_pool_kernel, S),
        grid=(S // TM,),
        in_specs=[
            pl.BlockSpec((TM, D_MODEL), lambda i: (i, 0)),
            pl.BlockSpec((POOL_HALO, D_MODEL), lambda i: (jnp.maximum(i * hb - 1, 0), 0)),
            pl.BlockSpec((POOL_HALO, D_MODEL), lambda i: (jnp.minimum((i + 1) * hb, n_halo - 1), 0)),
            pl.BlockSpec((SUBLANES, D_MODEL), lambda i: (0, 0)),
            pl.BlockSpec((len(POOL_WINDOWS), POOL_GROUP, POOL_GROUP), lambda i: (0, 0, 0)),
            pl.BlockSpec((SUBLANES, D_MODEL), lambda i: (0, 0)),
            pl.BlockSpec((D_MODEL, 2 * LANES), lambda i: (0, 0)),
            pl.BlockSpec((1, LANES), lambda i: (0, 0)),
        ],
        out_specs=_router_out_specs(),
        out_shape=_router_out_shapes(S),
        scratch_shapes=[
            pltpu.VMEM((SUBLANES, LANES), jnp.float32),
            pltpu.VMEM((TM + 2 * POOL_HALO, D_MODEL), jnp.float32),
        ],
        compiler_params=_cparams(("arbitrary",)),
        name="pool_router",
    )(x, x, x, mvecs, pool_w, vecs, rw, rb)


def _dispatch_kernel(pos_ref, ztile_ref, hf_ref, xs_ref, stage_ref, zeros_ref, load_sem, row_sem, zero_sem):
    i = pl.program_id(0)
    n = pl.num_programs(0)
    tile_rows = TM * SUBLANES
    pairs = TM * TOP_K

    def load(tile, slot):
        src = pl.multiple_of(tile * tile_rows, tile_rows)
        dst = pl.multiple_of(slot * tile_rows, tile_rows)
        return pltpu.make_async_copy(hf_ref.at[pl.ds(src, tile_rows)],
                                     stage_ref.at[pl.ds(dst, tile_rows)], load_sem.at[slot])

    def rows_wait():
        pltpu.make_async_copy(xs_ref.at[pl.ds(0, pairs * SUBLANES)],
                              xs_ref.at[pl.ds(0, pairs * SUBLANES)], row_sem).wait()

    def zero_copy(e):
        zero_rows = EM * SUBLANES
        start = pl.multiple_of(jnp.maximum(ztile_ref[e], 0) * zero_rows, zero_rows)
        return pltpu.make_async_copy(zeros_ref, xs_ref.at[pl.ds(start, zero_rows)], zero_sem)

    @pl.when(i == 0)
    def _():
        zeros_ref[...] = jnp.zeros_like(zeros_ref)
        load(0, 0).start()

        def zero_start(e, carry):
            @pl.when(ztile_ref[e] >= 0)
            def _():
                zero_copy(e).start()
            return carry

        def zero_wait(e, carry):
            @pl.when(ztile_ref[e] >= 0)
            def _():
                zero_copy(e).wait()
            return carry

        lax.fori_loop(0, 2 * N_EXPERTS, zero_start, 0)
        lax.fori_loop(0, 2 * N_EXPERTS, zero_wait, 0)

    @pl.when(i > 0)
    def _():
        rows_wait()

    @pl.when(i + 1 < n)
    def _():
        load(i + 1, (i + 1) % 2).start()

    slot = i % 2
    load(i, slot).wait()

    def issue(r, carry):
        src = pl.multiple_of(slot * tile_rows + r * SUBLANES, SUBLANES)
        for k in range(TOP_K):
            dst = pl.multiple_of(pos_ref[k * (n * TM) + i * TM + r] * SUBLANES, SUBLANES)
            pltpu.make_async_copy(stage_ref.at[pl.ds(src, SUBLANES)],
                                  xs_ref.at[pl.ds(dst, SUBLANES)], row_sem).start(priority=k % 2)
        return carry

    lax.fori_loop(0, TM, issue, 0)

    @pl.when(i == n - 1)
    def _():
        rows_wait()


def _dispatch(pos, ztile, hf_tiles, n_rows):
    S = hf_tiles.shape[0] // SUBLANES
    tile_rows = TM * SUBLANES
    return pl.pallas_call(
        _dispatch_kernel,
        grid_spec=pltpu.PrefetchScalarGridSpec(
            num_scalar_prefetch=2,
            grid=(S // TM,),
            in_specs=[pl.BlockSpec(memory_space=pl.ANY)],
            out_specs=pl.BlockSpec(memory_space=pl.ANY),
            scratch_shapes=[
                pltpu.VMEM((2 * tile_rows, LANES), jnp.float32),
                pltpu.VMEM((EM * SUBLANES, LANES), jnp.float32),
                pltpu.SemaphoreType.DMA((2,)),
                pltpu.SemaphoreType.DMA(()),
                pltpu.SemaphoreType.DMA(()),
            ],
        ),
        out_shape=jax.ShapeDtypeStruct((n_rows * SUBLANES, LANES), jnp.float32),
        compiler_params=_cparams(("arbitrary",)),
        name="moe_dispatch",
    )(pos, ztile, hf_tiles)


W_CAST_ROWS = 64


def _expert_kernel(te_ref, nused_ref, xs_ref, wgu_ref, bgu_ref, wd_ref, bd_ref, ys_ref,
                   wgu_bf_ref, wd_bf_ref):
    i = pl.program_id(0)
    used = i < nused_ref[0]
    new_expert = (i == 0) | (te_ref[i] != te_ref[jnp.maximum(i - 1, 0)])

    @pl.when(used & new_expert)
    def _():
        def cast_gu(r, carry):
            rows = pl.ds(pl.multiple_of(r * W_CAST_ROWS, W_CAST_ROWS), W_CAST_ROWS)
            wgu_bf_ref[rows, :] = wgu_ref[rows, :].astype(jnp.bfloat16)
            return carry

        def cast_d(r, carry):
            rows = pl.ds(pl.multiple_of(r * W_CAST_ROWS, W_CAST_ROWS), W_CAST_ROWS)
            wd_bf_ref[rows, :] = wd_ref[rows, :].astype(jnp.bfloat16)
            return carry

        lax.fori_loop(0, D_MODEL // W_CAST_ROWS, cast_gu, 0)
        lax.fori_loop(0, D_FF // W_CAST_ROWS, cast_d, 0)

    @pl.when(used)
    def _():
        cols = [xs_ref[pl.ds(c, EM, stride=SUBLANES), :] for c in range(D_CHUNKS)]
        x = jnp.concatenate(cols, axis=1).astype(jnp.bfloat16)
        gu = jnp.dot(x, wgu_bf_ref[...], preferred_element_type=jnp.float32) + bgu_ref[...]
        gate = jnp.minimum(gu[:, :D_FF], SWIGLU_LIMIT)
        up = jnp.clip(gu[:, D_FF:], -SWIGLU_LIMIT, SWIGLU_LIMIT)
        glu = gate / (1.0 + jnp.exp(-SWIGLU_ALPHA * gate))
        a = ((up + 1.0) * glu).astype(jnp.bfloat16)
        y = jnp.dot(a, wd_bf_ref[...], preferred_element_type=jnp.float32) + bd_ref[...]
        for c in range(D_CHUNKS):
            ys_ref[pl.ds(c, EM, stride=SUBLANES), :] = y[:, c * LANES:(c + 1) * LANES]

    @pl.when(jnp.logical_not(used))
    def _():
        ys_ref[...] = jnp.zeros_like(ys_ref)


def _experts(layer, tile_expert, n_used, xs, w_gu, b_gu, w_down, b_down):
    n_tiles = tile_expert.shape[0]
    L, E = w_gu.shape[:2]

    def row_map(i, te, nu):
        return (jnp.minimum(i, nu[0] - 1), 0)

    def w_map(i, te, nu):
        return (layer, te[i], 0, 0)

    return pl.pallas_call(
        _expert_kernel,
        grid_spec=pltpu.PrefetchScalarGridSpec(
            num_scalar_prefetch=2,
            grid=(n_tiles,),
            in_specs=[
                pl.BlockSpec((EM * SUBLANES, LANES), row_map),
                pl.BlockSpec((None, None, D_MODEL, 2 * D_FF), w_map),
                pl.BlockSpec((None, None, 1, 2 * D_FF), w_map),
                pl.BlockSpec((None, None, D_FF, D_MODEL), w_map),
                pl.BlockSpec((None, None, 1, D_MODEL), w_map),
            ],
            out_specs=pl.BlockSpec((EM * SUBLANES, LANES), lambda i, te, nu: (i, 0)),
            scratch_shapes=[
                pltpu.VMEM((D_MODEL, 2 * D_FF), jnp.bfloat16),
                pltpu.VMEM((D_FF, D_MODEL), jnp.bfloat16),
            ],
        ),
        out_shape=jax.ShapeDtypeStruct(xs.shape, jnp.float32),
        compiler_params=_cparams(("arbitrary",)),
        name="moe_experts",
    )(tile_expert, n_used, xs, w_gu, b_gu.reshape(L, E, 1, 2 * D_FF), w_down,
      b_down.reshape(L, E, 1, D_MODEL))


def _combine_kernel(final, pos_ref, ys_ref, x_ref, gate_ref, vec_ref, o_ref, buf_ref, sem):
    i = pl.program_id(0)
    n = pl.num_programs(0)
    slot_rows = TOP_K * TM * SUBLANES

    def issue(tile, slot):
        def one(r, carry):
            for k in range(TOP_K):
                src = pl.multiple_of(pos_ref[k * (n * TM) + tile * TM + r] * SUBLANES, SUBLANES)
                dst = pl.multiple_of(slot * slot_rows + (k * TM + r) * SUBLANES, SUBLANES)
                pltpu.make_async_copy(ys_ref.at[pl.ds(src, SUBLANES)],
                                      buf_ref.at[pl.ds(dst, SUBLANES)], sem.at[slot]).start(priority=k % 2)
            return carry

        lax.fori_loop(0, TM, one, 0)

    @pl.when(i == 0)
    def _():
        issue(0, 0)

    @pl.when(i + 1 < n)
    def _():
        issue(i + 1, (i + 1) % 2)

    slot = i % 2
    base = pl.multiple_of(slot * slot_rows, slot_rows)
    pltpu.make_async_copy(ys_ref.at[pl.ds(0, slot_rows)], buf_ref.at[pl.ds(base, slot_rows)],
                          sem.at[slot]).wait()

    gates = gate_ref[...]
    pieces = []
    for c in range(D_CHUNKS):
        f = None
        for k in range(TOP_K):
            rows = buf_ref[pl.ds(base + k * TM * SUBLANES + c, TM, stride=SUBLANES), :]
            term = gates[:, k:k + 1] * rows
            f = term if f is None else f + term
        pieces.append(f)
    f = jnp.concatenate(pieces, axis=1)
    x_new = x_ref[...] + vec_ref[0:1, :] * f
    if final:
        x_new = _rms(x_new) * vec_ref[1:2, :]
    o_ref[...] = x_new


def _combine(pos, ys, x, gates, vecs, final):
    S = x.shape[0]
    return pl.pallas_call(
        functools.partial(_combine_kernel, final),
        grid_spec=pltpu.PrefetchScalarGridSpec(
            num_scalar_prefetch=1,
            grid=(S // TM,),
            in_specs=[
                pl.BlockSpec(memory_space=pl.ANY),
                pl.BlockSpec((TM, D_MODEL), lambda i, p: (i, 0)),
                pl.BlockSpec((TM, LANES), lambda i, p: (i, 0)),
                pl.BlockSpec((SUBLANES, D_MODEL), lambda i, p: (0, 0)),
            ],
            out_specs=pl.BlockSpec((TM, D_MODEL), lambda i, p: (i, 0)),
            scratch_shapes=[
                pltpu.VMEM((2 * TOP_K * TM * SUBLANES, LANES), jnp.float32),
                pltpu.SemaphoreType.DMA((2,)),
            ],
        ),
        out_shape=jax.ShapeDtypeStruct((S, D_MODEL), jnp.float32),
        compiler_params=_cparams(("arbitrary",)),
        name="moe_combine",
    )(pos, ys, x, gates, vecs)


def _moe(layer, x_new, hf_tiles, meta, gates, counts, w_gu, b_gu, w_down, b_down, out_vecs, final):
    S = x_new.shape[0]
    n_tiles = (S * TOP_K) // EM + N_EXPERTS
    cnt = counts[0, :N_EXPERTS]
    padded = ((cnt + EM - 1) // EM) * EM
    ends = jnp.cumsum(padded)
    offs = ends - padded
    eid = meta[:TOP_K]
    rank = meta[TOP_K:2 * TOP_K]
    pos = (offs[eid] + rank).reshape(-1).astype(jnp.int32)
    tile_ends = ends // EM
    tile_ids = jnp.arange(n_tiles, dtype=jnp.int32)
    tile_expert = jnp.minimum(
        jnp.sum((tile_ends[None, :] <= tile_ids[:, None]).astype(jnp.int32), axis=1),
        N_EXPERTS - 1).astype(jnp.int32)
    n_used = tile_ends[-1:].astype(jnp.int32)
    trailing = tile_ends[-1] + jnp.arange(N_EXPERTS, dtype=jnp.int32)
    ztile = jnp.concatenate([
        jnp.where(padded > 0, tile_ends - 1, -1),
        jnp.where(trailing < n_tiles, trailing, -1)]).astype(jnp.int32)
    xs = _dispatch(pos, ztile, hf_tiles, n_tiles * EM)
    ys = _experts(layer, tile_expert, n_used, xs, w_gu, b_gu, w_down, b_down)
    return _combine(pos, ys, x_new, gates, out_vecs, final)


def _pad_rows(rows):
    rows = [r.reshape(1, -1) for r in rows]
    rows += [jnp.zeros_like(rows[0])] * (SUBLANES - len(rows))
    return jnp.concatenate(rows, axis=0)


def _rope_tables(S, C):
    quarter = HEAD_DIM // 4
    inv_freq = ROPE_THETA ** (-np.arange(quarter, dtype=np.float32) / quarter)
    n_rows = S // GRID_W
    ang_r = np.arange(n_rows, dtype=np.float32)[:, None] * inv_freq[None, :]
    ang_c = np.arange(GRID_W, dtype=np.float32)[:, None] * inv_freq[None, :]

    def expand(tab_r, tab_c, sign):
        r = jnp.repeat(jnp.asarray(tab_r, jnp.float32), GRID_W, axis=0)
        cc = jnp.tile(jnp.asarray(tab_c, jnp.float32), (n_rows, 1))
        return jnp.concatenate([sign * r, r, sign * cc, cc], axis=1)

    cos = expand(np.cos(ang_r), np.cos(ang_c), 1.0)
    sin = expand(np.sin(ang_r), np.sin(ang_c), -1.0)
    cos = jnp.concatenate([cos, jnp.ones((C, HEAD_DIM), jnp.float32)], axis=0)
    sin = jnp.concatenate([sin, jnp.zeros((C, HEAD_DIM), jnp.float32)], axis=0)
    return cos, sin


def kernel(x, c, ctx, c_ctx, ada_w, ada_b, norm_mix, norm_ffn, attn_w_qkv, attn_q_norm, attn_k_norm,
           attn_w_o, pool_w, pool_scale, moe_router_w, moe_router_b, moe_w_gu, moe_b_gu, moe_w_down,
           moe_b_down, final_norm):
    B, S, D = x.shape
    C = ctx.shape[1]
    assert B == 1 and D == D_MODEL and S % TM == 0 and C % TM == 0 and S % GRID_W == 0
    x2d = x.reshape(S, D)
    ctx2d = ctx.reshape(C, D)
    bf = jnp.bfloat16

    cvec = _pad_rows([c.reshape(-1), c_ctx])
    mod = _modulation(cvec, ada_w, ada_b)
    m_l = [mod[l, 0].reshape(N_MOD, D) for l in range(2)]
    m_c0 = mod[0, 1].reshape(N_MOD, D)

    def split_router(w):
        w = jnp.pad(w, ((0, 0), (0, LANES - N_EXPERTS)))
        hi = w.astype(bf)
        lo = (w - hi.astype(jnp.float32)).astype(bf)
        return jnp.concatenate([hi, lo], axis=1)

    rw = [split_router(moe_router_w[l]) for l in range(2)]
    rb = [jnp.pad(moe_router_b[l], (0, LANES - N_EXPERTS), constant_values=NEG_BIG).reshape(1, LANES)
          for l in range(2)]

    cos_t, sin_t = _rope_tables(S, C)
    qkv_vecs = _pad_rows([m_l[0][0], m_l[0][1], m_c0[0], m_c0[1], norm_mix[0]])
    q, k, v = _qkv(x2d, ctx2d, qkv_vecs, attn_w_qkv[0].astype(bf), attn_q_norm[0].reshape(1, -1),
                   attn_k_norm[0].reshape(1, -1), cos_t, sin_t)
    o = _attention(q, k, v, S)
    vecs0 = _pad_rows([m_l[0][2], m_l[0][3], m_l[0][4], norm_ffn[0]])
    x1, hf, meta, gates, counts = _post_attn(x2d, o, attn_w_o[0].astype(bf), vecs0, rw[0], rb[0])
    x2 = _moe(0, x1, hf, meta, gates, counts, moe_w_gu, moe_b_gu, moe_w_down, moe_b_down,
              _pad_rows([m_l[0][5]]), final=False)

    mvecs = _pad_rows([m_l[1][0], m_l[1][1], norm_mix[1], pool_scale[0]])
    vecs1 = _pad_rows([m_l[1][2], m_l[1][3], m_l[1][4], norm_ffn[1]])
    x3, hf, meta, gates, counts = _pool(x2, mvecs, pool_w[0].astype(bf), vecs1, rw[1], rb[1])
    out = _moe(1, x3, hf, meta, gates, counts, moe_w_gu, moe_b_gu, moe_w_down, moe_b_down,
               _pad_rows([m_l[1][5], final_norm]), final=True)
    return out.reshape(B, S, D)
```

```python
import functools

import numpy as np
import jax
import jax.numpy as jnp
from jax import lax
from jax.experimental import pallas as pl
from jax.experimental.pallas import tpu as pltpu

D_MODEL = 1024
GRID_W = 64
N_HEADS = 8
N_KV_HEADS = 2
KV_GROUP = N_HEADS // N_KV_HEADS
HEAD_DIM = 128
ROPE_THETA = 10000.0
POOL_WINDOWS = (2, 4, 8, 16)
POOL_GROUP = D_MODEL // len(POOL_WINDOWS)
POOL_HALO = 8
N_EXPERTS = 32
TOP_K = 4
D_FF = D_MODEL
SWIGLU_LIMIT = 7.0
SWIGLU_ALPHA = 1.702
NORM_EPS = 1e-6
N_MOD = 6

LANES = 128
SUBLANES = 8
D_CHUNKS = D_MODEL // LANES
TM = 256
EM = 512
RM = 512
TK_MAX = 1280
ATTN_ROW_SUM_FLOOR = 2.0 ** -100
ATTN_UNROLL = 2
VMEM_LIMIT = 48 * 1024 * 1024
ATTN_VMEM_LIMIT = 56 * 1024 * 1024
LOG2E = 1.4426950408889634
NEG_BIG = -1e30

_HI = lax.Precision.HIGHEST


def _cparams(sem):
    return pltpu.CompilerParams(dimension_semantics=sem, vmem_limit_bytes=VMEM_LIMIT)


def _rms(x, eps=NORM_EPS):
    return x * lax.rsqrt(jnp.mean(x * x, axis=-1, keepdims=True) + eps)


def _mod_kernel(cv_ref, w_ref, b_ref, o_ref):
    cv = cv_ref[...]
    s = cv / (1.0 + jnp.exp(-cv))
    o_ref[0] = jnp.dot(s, w_ref[0], precision=_HI, preferred_element_type=jnp.float32) + b_ref[0]


def _modulation(cvec, ada_w, ada_b):
    depth = ada_w.shape[0]
    nblk = 4
    bw = N_MOD * D_MODEL // nblk
    return pl.pallas_call(
        _mod_kernel,
        grid=(depth, nblk),
        in_specs=[
            pl.BlockSpec((SUBLANES, D_MODEL), lambda l, j: (0, 0)),
            pl.BlockSpec((1, D_MODEL, bw), lambda l, j: (l, 0, j)),
            pl.BlockSpec((1, 1, bw), lambda l, j: (l, 0, j)),
        ],
        out_specs=pl.BlockSpec((1, SUBLANES, bw), lambda l, j: (l, 0, j)),
        out_shape=jax.ShapeDtypeStruct((depth, SUBLANES, N_MOD * D_MODEL), jnp.float32),
        compiler_params=_cparams(("arbitrary", "arbitrary")),
        name="modulation",
    )(cvec, ada_w, ada_b.reshape(depth, 1, N_MOD * D_MODEL))


def _rope(x, cos, sin_signed):
    lane = lax.broadcasted_iota(jnp.int32, x.shape, 1)
    partner = jnp.where((lane % 64) < 32, pltpu.roll(x, LANES - 32, 1), pltpu.roll(x, 32, 1))
    return x * cos + partner * sin_signed


def _qkv_kernel(n_lat_tiles, x_ref, ctx_ref, vec_ref, w_ref, qg_ref, kg_ref, cos_ref, sin_ref,
                q_ref, k_ref, v_ref):
    i = pl.program_id(0)
    is_ctx = i >= n_lat_tiles
    xt = jnp.where(is_ctx, ctx_ref[...], x_ref[...])
    sh = jnp.where(is_ctx, vec_ref[2:3, :], vec_ref[0:1, :])
    sc = jnp.where(is_ctx, vec_ref[3:4, :], vec_ref[1:2, :])
    h = _rms(xt) * vec_ref[4:5, :] * (1.0 + sc) + sh
    qkv = jnp.dot(h.astype(jnp.bfloat16), w_ref[...], preferred_element_type=jnp.float32)
    cos = cos_ref[...]
    sin = sin_ref[...]
    q_scale = (HEAD_DIM ** -0.5) * LOG2E
    for hd in range(N_HEADS):
        qh = _rms(qkv[:, hd * HEAD_DIM:(hd + 1) * HEAD_DIM]) * qg_ref[...]
        q_ref[hd] = (_rope(qh, cos, sin) * q_scale).astype(jnp.bfloat16)
    for kv in range(N_KV_HEADS):
        c0 = (N_HEADS + kv) * HEAD_DIM
        kh = _rms(qkv[:, c0:c0 + HEAD_DIM]) * kg_ref[...]
        k_ref[kv] = _rope(kh, cos, sin).astype(jnp.bfloat16)
        c1 = (N_HEADS + N_KV_HEADS + kv) * HEAD_DIM
        v_ref[kv] = qkv[:, c1:c1 + HEAD_DIM].astype(jnp.bfloat16)


def _qkv(x, ctx, vecs, w_qkv, q_g, k_g, cos_t, sin_t):
    S, C = x.shape[0], ctx.shape[0]
    n_lat = S // TM
    n_ctx = C // TM
    T = S + C
    last = n_lat - 1
    qkv_w = w_qkv.shape[1]
    out_shapes = (
        jax.ShapeDtypeStruct((N_HEADS, T, HEAD_DIM), jnp.bfloat16),
        jax.ShapeDtypeStruct((N_KV_HEADS, T, HEAD_DIM), jnp.bfloat16),
        jax.ShapeDtypeStruct((N_KV_HEADS, T, HEAD_DIM), jnp.bfloat16),
    )
    return pl.pallas_call(
        functools.partial(_qkv_kernel, n_lat),
        grid=(n_lat + n_ctx,),
        in_specs=[
            pl.BlockSpec((TM, D_MODEL), lambda i: (jnp.minimum(i, last), 0)),
            pl.BlockSpec((TM, D_MODEL), lambda i: (jnp.maximum(i - n_lat, 0), 0)),
            pl.BlockSpec((SUBLANES, D_MODEL), lambda i: (0, 0)),
            pl.BlockSpec((D_MODEL, qkv_w), lambda i: (0, 0)),
            pl.BlockSpec((1, HEAD_DIM), lambda i: (0, 0)),
            pl.BlockSpec((1, HEAD_DIM), lambda i: (0, 0)),
            pl.BlockSpec((TM, HEAD_DIM), lambda i: (i, 0)),
            pl.BlockSpec((TM, HEAD_DIM), lambda i: (i, 0)),
        ],
        out_specs=(
            pl.BlockSpec((N_HEADS, TM, HEAD_DIM), lambda i: (0, i, 0)),
            pl.BlockSpec((N_KV_HEADS, TM, HEAD_DIM), lambda i: (0, i, 0)),
            pl.BlockSpec((N_KV_HEADS, TM, HEAD_DIM), lambda i: (0, i, 0)),
        ),
        out_shape=out_shapes,
        compiler_params=_cparams(("arbitrary",)),
        name="qkv_proj",
    )(x, ctx, vecs, w_qkv, q_g, k_g, cos_t, sin_t)


def _lane_repeat(x, n):
    return jnp.concatenate([x] * n, axis=1)


def _attn_kernel(n_chunks, tk, q_ref, k_ref, v_ref, o_ref, s_ref, p_ref, a_ref, mc_ref, m_ref, acc_ref):
    rows = KV_GROUP * TM
    ones = jnp.ones((tk, HEAD_DIM), jnp.bfloat16)

    def scores(c, slot):
        k = k_ref[0, pl.ds(pl.multiple_of(c * tk, tk), tk), :]
        q = q_ref[...].reshape(rows, HEAD_DIM)
        s = lax.dot_general(q, k, (((1,), (1,)), ((), ())), preferred_element_type=jnp.float32)
        s_ref[slot] = s
        mc_ref[slot] = jnp.broadcast_to(jnp.max(s, axis=1, keepdims=True), (rows, LANES))

    def softmax(slot):
        m_old = m_ref[...]
        m_new = jnp.maximum(m_old, mc_ref[slot])
        a_ref[slot] = jnp.exp2(m_old - m_new)
        m_ref[...] = m_new
        p_ref[slot] = jnp.exp2(s_ref[slot] - _lane_repeat(m_new, tk // LANES)).astype(jnp.bfloat16)

    def values(c, slot):
        v = v_ref[0, pl.ds(pl.multiple_of(c * tk, tk), tk), :]
        vx = jnp.concatenate([v, ones], axis=1)
        acc_ref[...] = (acc_ref[...] * _lane_repeat(a_ref[slot], 2)
                        + jnp.dot(p_ref[slot], vx, preferred_element_type=jnp.float32))

    def iteration(t, par):
        scores(t + 1, 1 - par)
        softmax(par)
        values(jnp.maximum(t - 1, 0), 1 - par)

    m_ref[...] = jnp.full(m_ref.shape, -jnp.inf, jnp.float32)
    acc_ref[...] = jnp.zeros(acc_ref.shape, jnp.float32)
    p_ref[1] = jnp.zeros(p_ref.shape[1:], jnp.bfloat16)
    a_ref[1] = jnp.ones(a_ref.shape[1:], jnp.float32)
    scores(0, 0)

    n_steady = n_chunks - 1

    def group(u, carry):
        for j in range(ATTN_UNROLL):
            iteration(ATTN_UNROLL * u + j, j % 2)
        return carry

    n_groups = n_steady // ATTN_UNROLL
    lax.fori_loop(0, n_groups, group, 0)
    for t in range(n_groups * ATTN_UNROLL, n_steady):
        iteration(t, t % 2)
    last = n_chunks - 1
    softmax(last % 2)
    values(max(last - 1, 0), 1 - last % 2)
    values(last, last % 2)

    acc = acc_ref[...]
    o = acc[:, :HEAD_DIM] / acc[:, HEAD_DIM:]
    for g in range(KV_GROUP):
        o_ref[:, g * HEAD_DIM:(g + 1) * HEAD_DIM] = o[g * TM:(g + 1) * TM].astype(o_ref.dtype)


def _key_chunk(T):
    return max(c for c in range(LANES, TK_MAX + 1, LANES) if T % c == 0)


def _attn_bound_kernel(n_chunks, tk, q_ref, k_ref, v_ref, o_ref, lmin_ref, p_ref, m_ref, acc_ref, kmax_ref):
    i = pl.program_id(1)
    rows = KV_GROUP * TM
    ones = jnp.ones((tk, HEAD_DIM), jnp.bfloat16)

    @pl.when(i == 0)
    def _():
        def body(c, mx):
            k = k_ref[0, pl.ds(pl.multiple_of(c * tk, tk), tk), :].astype(jnp.float32)
            n2 = jnp.sum(k * k, axis=1, keepdims=True)
            return jnp.maximum(mx, jnp.max(n2, axis=0, keepdims=True))

        kmax2 = lax.fori_loop(0, n_chunks, body, jnp.zeros((1, 1), jnp.float32))
        kmax_ref[...] = jnp.broadcast_to(kmax2, kmax_ref.shape)

    qf = q_ref[...].reshape(rows, HEAD_DIM).astype(jnp.float32)
    qn2 = jnp.sum(qf * qf, axis=1, keepdims=True)
    m_ref[...] = jnp.broadcast_to(jnp.sqrt(qn2 * kmax_ref[0:1, 0:1]), (rows, LANES))
    acc_ref[...] = jnp.zeros(acc_ref.shape, jnp.float32)

    def probs(c, slot):
        k = k_ref[0, pl.ds(pl.multiple_of(c * tk, tk), tk), :]
        q = q_ref[...].reshape(rows, HEAD_DIM)
        s = lax.dot_general(q, k, (((1,), (1,)), ((), ())), preferred_element_type=jnp.float32)
        p_ref[slot] = jnp.exp2(s - _lane_repeat(m_ref[...], tk // LANES)).astype(jnp.bfloat16)

    def values(c, slot):
        v = v_ref[0, pl.ds(pl.multiple_of(c * tk, tk), tk), :]
        vx = jnp.concatenate([v, ones], axis=1)
        acc_ref[...] += jnp.dot(p_ref[slot], vx, preferred_element_type=jnp.float32)

    def iteration(t, par):
        probs(t + 1, 1 - par)
        values(t, par)

    probs(0, 0)
    n_steady = n_chunks - 1

    def group(u, carry):
        for j in range(ATTN_UNROLL):
            iteration(ATTN_UNROLL * u + j, j % 2)
        return carry

    n_groups = n_steady // ATTN_UNROLL
    lax.fori_loop(0, n_groups, group, 0)
    for t in range(n_groups * ATTN_UNROLL, n_steady):
        iteration(t, t % 2)
    values(n_chunks - 1, (n_chunks - 1) % 2)

    acc = acc_ref[...]
    l = acc[:, HEAD_DIM:]
    o = acc[:, :HEAD_DIM] / l
    for g in range(KV_GROUP):
        o_ref[:, g * HEAD_DIM:(g + 1) * HEAD_DIM] = o[g * TM:(g + 1) * TM].astype(o_ref.dtype)
    lmin_ref[0] = jnp.broadcast_to(jnp.min(l, axis=0, keepdims=True), (SUBLANES, LANES))


def _attention_bound(q, k, v, S):
    T = k.shape[1]
    tk = _key_chunk(T)
    rows = KV_GROUP * TM
    n_q = S // TM
    return pl.pallas_call(
        functools.partial(_attn_bound_kernel, T // tk, tk),
        grid=(N_KV_HEADS, n_q),
        in_specs=[
            pl.BlockSpec((KV_GROUP, TM, HEAD_DIM), lambda g, i: (g, i, 0)),
            pl.BlockSpec((1, T, HEAD_DIM), lambda g, i: (g, 0, 0)),
            pl.BlockSpec((1, T, HEAD_DIM), lambda g, i: (g, 0, 0)),
        ],
        out_specs=(
            pl.BlockSpec((TM, KV_GROUP * HEAD_DIM), lambda g, i: (i, g)),
            pl.BlockSpec((1, SUBLANES, LANES), lambda g, i: (g * n_q + i, 0, 0)),
        ),
        out_shape=(
            jax.ShapeDtypeStruct((S, N_HEADS * HEAD_DIM), jnp.bfloat16),
            jax.ShapeDtypeStruct((N_KV_HEADS * n_q, SUBLANES, LANES), jnp.float32),
        ),
        scratch_shapes=[
            pltpu.VMEM((2, rows, tk), jnp.bfloat16),
            pltpu.VMEM((rows, LANES), jnp.float32),
            pltpu.VMEM((rows, 2 * HEAD_DIM), jnp.float32),
            pltpu.VMEM((SUBLANES, LANES), jnp.float32),
        ],
        compiler_params=pltpu.CompilerParams(dimension_semantics=("arbitrary", "arbitrary"),
                                             vmem_limit_bytes=ATTN_VMEM_LIMIT),
        name="flash_attention_bound",
    )(q, k, v)


def _attention(q, k, v, S):
    o, lmin = _attention_bound(q, k, v, S)
    ok = jnp.min(lmin) >= ATTN_ROW_SUM_FLOOR
    return lax.cond(ok, lambda: o, lambda: _attention_running_max(q, k, v, S))


def _attention_running_max(q, k, v, S):
    T = k.shape[1]
    tk = _key_chunk(T)
    rows = KV_GROUP * TM
    return pl.pallas_call(
        functools.partial(_attn_kernel, T // tk, tk),
        grid=(N_KV_HEADS, S // TM),
        in_specs=[
            pl.BlockSpec((KV_GROUP, TM, HEAD_DIM), lambda g, i: (g, i, 0)),
            pl.BlockSpec((1, T, HEAD_DIM), lambda g, i: (g, 0, 0)),
            pl.BlockSpec((1, T, HEAD_DIM), lambda g, i: (g, 0, 0)),
        ],
        out_specs=pl.BlockSpec((TM, KV_GROUP * HEAD_DIM), lambda g, i: (i, g)),
        out_shape=jax.ShapeDtypeStruct((S, N_HEADS * HEAD_DIM), jnp.bfloat16),
        scratch_shapes=[
            pltpu.VMEM((2, rows, tk), jnp.float32),
            pltpu.VMEM((2, rows, tk), jnp.bfloat16),
            pltpu.VMEM((2, rows, LANES), jnp.float32),
            pltpu.VMEM((2, rows, LANES), jnp.float32),
            pltpu.VMEM((rows, LANES), jnp.float32),
            pltpu.VMEM((rows, 2 * HEAD_DIM), jnp.float32),
        ],
        compiler_params=pltpu.CompilerParams(dimension_semantics=("arbitrary", "arbitrary"),
                                             vmem_limit_bytes=ATTN_VMEM_LIMIT),
        name="flash_attention",
    )(q, k, v)


def _router_out_shapes(S):
    return (
        jax.ShapeDtypeStruct((S, D_MODEL), jnp.float32),
        jax.ShapeDtypeStruct((S * SUBLANES, LANES), jnp.float32),
        jax.ShapeDtypeStruct((SUBLANES, S), jnp.int32),
        jax.ShapeDtypeStruct((S, LANES), jnp.float32),
        jax.ShapeDtypeStruct((SUBLANES, LANES), jnp.int32),
    )


def _router_out_specs():
    return (
        pl.BlockSpec((RM, D_MODEL), lambda i: (i, 0)),
        pl.BlockSpec((RM * SUBLANES, LANES), lambda i: (i, 0)),
        pl.BlockSpec((SUBLANES, RM), lambda i: (0, i)),
        pl.BlockSpec((RM, LANES), lambda i: (i, 0)),
        pl.BlockSpec((SUBLANES, LANES), lambda i: (0, 0)),
    )


def _router_core(x_new, vec_ref, rw_ref, rb_ref, cnt_ref, x_out_ref, hf_ref, meta_ref, gate_ref,
                 counts_ref):
    i = pl.program_id(0)

    @pl.when(i == 0)
    def _():
        cnt_ref[...] = jnp.zeros_like(cnt_ref)

    x_out_ref[...] = x_new
    hf = _rms(x_new) * vec_ref[3:4, :] * (1.0 + vec_ref[2:3, :]) + vec_ref[1:2, :]
    for c in range(D_CHUNKS):
        hf_ref[pl.ds(c, RM, stride=SUBLANES), :] = hf[:, c * LANES:(c + 1) * LANES]

    hf_hi = hf.astype(jnp.bfloat16)
    hf_lo = (hf - hf_hi.astype(jnp.float32)).astype(jnp.bfloat16)
    cross = (jnp.dot(hf_hi, rw_ref[...], preferred_element_type=jnp.float32)
             + jnp.dot(hf_lo, rw_ref[...], preferred_element_type=jnp.float32))
    logits = cross[:, :LANES] + cross[:, LANES:] + rb_ref[...]
    lane = lax.broadcasted_iota(jnp.int32, logits.shape, 1)
    lane_f = lane.astype(jnp.float32)
    vals, idxs, hots = [], [], []
    work = logits
    for _ in range(TOP_K):
        mx = jnp.max(work, axis=1, keepdims=True)
        idx = jnp.min(jnp.where(work == mx, lane_f, float(LANES)), axis=1, keepdims=True)
        hot = lane_f == idx
        work = jnp.where(hot, -jnp.inf, work)
        vals.append(mx)
        idxs.append(idx.astype(jnp.int32))
        hots.append(hot)
    exps = [jnp.exp(vk - vals[0]) for vk in vals]
    denom = exps[0] + exps[1] + exps[2] + exps[3]

    any_hot = hots[0] | hots[1] | hots[2] | hots[3]
    hot_f = jnp.where(any_hot, 1.0, 0.0)
    r_i = lax.broadcasted_iota(jnp.int32, (RM, RM), 0)
    c_i = lax.broadcasted_iota(jnp.int32, (RM, RM), 1)
    lower = jnp.where(c_i < r_i, 1.0, 0.0).astype(jnp.bfloat16)
    before = jnp.dot(lower, hot_f.astype(jnp.bfloat16), preferred_element_type=jnp.float32) + cnt_ref[0:1, :]
    cnt_ref[0:1, :] = cnt_ref[0:1, :] + jnp.sum(hot_f, axis=0, keepdims=True)

    meta = jnp.zeros(logits.shape, jnp.float32)
    gates = jnp.zeros(logits.shape, jnp.float32)
    for k in range(TOP_K):
        rank = jnp.sum(jnp.where(hots[k], before, 0.0), axis=1, keepdims=True)
        meta = jnp.where(lane == k, idxs[k].astype(jnp.float32), meta)
        meta = jnp.where(lane == TOP_K + k, rank, meta)
        gates = jnp.where(lane == k, exps[k] / denom, gates)
    meta_ref[...] = meta.T[:SUBLANES, :].astype(jnp.int32)
    gate_ref[...] = gates
    counts_ref[...] = jnp.broadcast_to(cnt_ref[0:1, :], counts_ref.shape).astype(jnp.int32)


def _post_attn_kernel(x_ref, o_ref, wo_ref, vec_ref, rw_ref, rb_ref,
                      x_out_ref, hf_ref, meta_ref, gate_ref, counts_ref, cnt_ref):
    y = jnp.dot(o_ref[...], wo_ref[...], preferred_element_type=jnp.float32)
    x_new = x_ref[...] + vec_ref[0:1, :] * y
    _router_core(x_new, vec_ref, rw_ref, rb_ref, cnt_ref, x_out_ref, hf_ref, meta_ref, gate_ref,
                 counts_ref)


def _post_attn(x, o, w_o, vecs, rw, rb):
    S = x.shape[0]
    return pl.pallas_call(
        _post_attn_kernel,
        grid=(S // RM,),
        in_specs=[
            pl.BlockSpec((RM, D_MODEL), lambda i: (i, 0)),
            pl.BlockSpec((RM, D_MODEL), lambda i: (i, 0)),
            pl.BlockSpec((D_MODEL, D_MODEL), lambda i: (0, 0)),
            pl.BlockSpec((SUBLANES, D_MODEL), lambda i: (0, 0)),
            pl.BlockSpec((D_MODEL, 2 * LANES), lambda i: (0, 0)),
            pl.BlockSpec((1, LANES), lambda i: (0, 0)),
        ],
        out_specs=_router_out_specs(),
        out_shape=_router_out_shapes(S),
        scratch_shapes=[pltpu.VMEM((SUBLANES, LANES), jnp.float32)],
        compiler_params=_cparams(("arbitrary",)),
        name="post_attn_router",
    )(x, o, w_o, vecs, rw, rb)


def _pool_kernel(S, x_ref, prev_ref, next_ref, mvec_ref, pw_ref, vec_ref, rw_ref, rb_ref,
                 x_out_ref, hf_ref, meta_ref, gate_ref, counts_ref, cnt_ref, buf_ref):
    i = pl.program_id(0)
    n = pl.num_programs(0)

    def hmod(xx):
        return _rms(xx) * mvec_ref[2:3, :] * (1.0 + mvec_ref[1:2, :]) + mvec_ref[0:1, :]

    x_cur = x_ref[...]
    h_cur = hmod(x_cur)
    buf_ref[0:POOL_HALO, :] = jnp.where(i > 0, hmod(prev_ref[...]), 0.0)
    buf_ref[POOL_HALO:POOL_HALO + RM, :] = h_cur
    buf_ref[POOL_HALO + RM:, :] = jnp.where(i < n - 1, hmod(next_ref[...]), 0.0)

    t = i * RM + lax.broadcasted_iota(jnp.int32, (RM, 1), 0)
    ys = []
    for g, w in enumerate(POOL_WINDOWS):
        c0 = g * POOL_GROUP
        acc = None
        for d in range(-(w // 2), w - w // 2):
            piece = buf_ref[POOL_HALO + d:POOL_HALO + d + RM, c0:c0 + POOL_GROUP]
            acc = piece if acc is None else acc + piece
        cnt = jnp.minimum(t - w // 2 + w, S) - jnp.maximum(t - w // 2, 0)
        pooled = acc / cnt.astype(jnp.float32) - h_cur[:, c0:c0 + POOL_GROUP]
        ys.append(jnp.dot(pooled.astype(jnp.bfloat16), pw_ref[g], preferred_element_type=jnp.float32))
    y = jnp.concatenate(ys, axis=1) * mvec_ref[3:4, :]
    x_new = x_cur + vec_ref[0:1, :] * y
    _router_core(x_new, vec_ref, rw_ref, rb_ref, cnt_ref, x_out_ref, hf_ref, meta_ref, gate_ref,
                 counts_ref)


def _pool(x, mvecs, pool_w, vecs, rw, rb):
    S = x.shape[0]
    hb = RM // POOL_HALO
    n_halo = S // POOL_HALO
    return pl.pallas_call(
        functools.partial(_pool_kernel, S),
        grid=(S // RM,),
        in_specs=[
            pl.BlockSpec((RM, D_MODEL), lambda i: (i, 0)),
            pl.BlockSpec((POOL_HALO, D_MODEL), lambda i: (jnp.maximum(i * hb - 1, 0), 0)),
            pl.BlockSpec((POOL_HALO, D_MODEL), lambda i: (jnp.minimum((i + 1) * hb, n_halo - 1), 0)),
            pl.BlockSpec((SUBLANES, D_MODEL), lambda i: (0, 0)),
            pl.BlockSpec((len(POOL_WINDOWS), POOL_GROUP, POOL_GROUP), lambda i: (0, 0, 0)),
            pl.BlockSpec((SUBLANES, D_MODEL), lambda i: (0, 0)),
            pl.BlockSpec((D_MODEL, 2 * LANES), lambda i: (0, 0)),
            pl.BlockSpec((1, LANES), lambda i: (0, 0)),
        ],
        out_specs=_router_out_specs(),
        out_shape=_router_out_shapes(S),
        scratch_shapes=[
            pltpu.VMEM((SUBLANES, LANES), jnp.float32),
            pltpu.VMEM((RM + 2 * POOL_HALO, D_MODEL), jnp.float32),
        ],
        compiler_params=_cparams(("arbitrary",)),
        name="pool_router",
    )(x, x, x, mvecs, pool_w, vecs, rw, rb)


def _dispatch_kernel(pos_ref, ztile_ref, hf_ref, xs_ref, stage_ref, zeros_ref, load_sem, row_sem, zero_sem):
    i = pl.program_id(0)
    n = pl.num_programs(0)
    tile_rows = TM * SUBLANES
    pairs = TM * TOP_K

    def load(tile, slot):
        src = pl.multiple_of(tile * tile_rows, tile_rows)
        dst = pl.multiple_of(slot * tile_rows, tile_rows)
        return pltpu.make_async_copy(hf_ref.at[pl.ds(src, tile_rows)],
                                     stage_ref.at[pl.ds(dst, tile_rows)], load_sem.at[slot])

    def rows_wait():
        pltpu.make_async_copy(xs_ref.at[pl.ds(0, pairs * SUBLANES)],
                              xs_ref.at[pl.ds(0, pairs * SUBLANES)], row_sem).wait()

    def zero_copy(e):
        zero_rows = EM * SUBLANES
        start = pl.multiple_of(jnp.maximum(ztile_ref[e], 0) * zero_rows, zero_rows)
        return pltpu.make_async_copy(zeros_ref, xs_ref.at[pl.ds(start, zero_rows)], zero_sem)

    @pl.when(i == 0)
    def _():
        zeros_ref[...] = jnp.zeros_like(zeros_ref)
        load(0, 0).start()

        def zero_start(e, carry):
            @pl.when(ztile_ref[e] >= 0)
            def _():
                zero_copy(e).start()
            return carry

        def zero_wait(e, carry):
            @pl.when(ztile_ref[e] >= 0)
            def _():
                zero_copy(e).wait()
            return carry

        lax.fori_loop(0, 2 * N_EXPERTS, zero_start, 0)
        lax.fori_loop(0, 2 * N_EXPERTS, zero_wait, 0)

    @pl.when(i > 0)
    def _():
        rows_wait()

    @pl.when(i + 1 < n)
    def _():
        load(i + 1, (i + 1) % 2).start()

    slot = i % 2
    load(i, slot).wait()

    def issue(r, carry):
        src = pl.multiple_of(slot * tile_rows + r * SUBLANES, SUBLANES)
        for k in range(TOP_K):
            dst = pl.multiple_of(pos_ref[k * (n * TM) + i * TM + r] * SUBLANES, SUBLANES)
            pltpu.make_async_copy(stage_ref.at[pl.ds(src, SUBLANES)],
                                  xs_ref.at[pl.ds(dst, SUBLANES)], row_sem).start(priority=k % 2)
        return carry

    lax.fori_loop(0, TM, issue, 0)

    @pl.when(i == n - 1)
    def _():
        rows_wait()


def _dispatch(pos, ztile, hf_tiles, n_rows):
    S = hf_tiles.shape[0] // SUBLANES
    tile_rows = TM * SUBLANES
    return pl.pallas_call(
        _dispatch_kernel,
        grid_spec=pltpu.PrefetchScalarGridSpec(
            num_scalar_prefetch=2,
            grid=(S // TM,),
            in_specs=[pl.BlockSpec(memory_space=pl.ANY)],
            out_specs=pl.BlockSpec(memory_space=pl.ANY),
            scratch_shapes=[
                pltpu.VMEM((2 * tile_rows, LANES), jnp.float32),
                pltpu.VMEM((EM * SUBLANES, LANES), jnp.float32),
                pltpu.SemaphoreType.DMA((2,)),
                pltpu.SemaphoreType.DMA(()),
                pltpu.SemaphoreType.DMA(()),
            ],
        ),
        out_shape=jax.ShapeDtypeStruct((n_rows * SUBLANES, LANES), jnp.float32),
        compiler_params=_cparams(("arbitrary",)),
        name="moe_dispatch",
    )(pos, ztile, hf_tiles)


W_CAST_ROWS = 64


def _expert_kernel(te_ref, nused_ref, xs_ref, wgu_ref, bgu_ref, wd_ref, bd_ref, ys_ref,
                   wgu_bf_ref, wd_bf_ref):
    i = pl.program_id(0)
    used = i < nused_ref[0]
    new_expert = (i == 0) | (te_ref[i] != te_ref[jnp.maximum(i - 1, 0)])

    @pl.when(used & new_expert)
    def _():
        def cast_gu(r, carry):
            rows = pl.ds(pl.multiple_of(r * W_CAST_ROWS, W_CAST_ROWS), W_CAST_ROWS)
            wgu_bf_ref[rows, :] = wgu_ref[rows, :].astype(jnp.bfloat16)
            return carry

        def cast_d(r, carry):
            rows = pl.ds(pl.multiple_of(r * W_CAST_ROWS, W_CAST_ROWS), W_CAST_ROWS)
            wd_bf_ref[rows, :] = wd_ref[rows, :].astype(jnp.bfloat16)
            return carry

        lax.fori_loop(0, D_MODEL // W_CAST_ROWS, cast_gu, 0)
        lax.fori_loop(0, D_FF // W_CAST_ROWS, cast_d, 0)

    @pl.when(used)
    def _():
        cols = [xs_ref[pl.ds(c, EM, stride=SUBLANES), :] for c in range(D_CHUNKS)]
        x = jnp.concatenate(cols, axis=1).astype(jnp.bfloat16)
        gu = jnp.dot(x, wgu_bf_ref[...], preferred_element_type=jnp.float32) + bgu_ref[...]
        gate = jnp.minimum(gu[:, :D_FF], SWIGLU_LIMIT)
        up = jnp.clip(gu[:, D_FF:], -SWIGLU_LIMIT, SWIGLU_LIMIT)
        glu = gate / (1.0 + jnp.exp(-SWIGLU_ALPHA * gate))
        a = ((up + 1.0) * glu).astype(jnp.bfloat16)
        y = jnp.dot(a, wd_bf_ref[...], preferred_element_type=jnp.float32) + bd_ref[...]
        for c in range(D_CHUNKS):
            ys_ref[pl.ds(c, EM, stride=SUBLANES), :] = y[:, c * LANES:(c + 1) * LANES]

    @pl.when(jnp.logical_not(used))
    def _():
        ys_ref[...] = jnp.zeros_like(ys_ref)


def _experts(layer, tile_expert, n_used, xs, w_gu, b_gu, w_down, b_down):
    n_tiles = tile_expert.shape[0]
    L, E = w_gu.shape[:2]

    def row_map(i, te, nu):
        return (jnp.minimum(i, nu[0] - 1), 0)

    def w_map(i, te, nu):
        return (layer, te[i], 0, 0)

    return pl.pallas_call(
        _expert_kernel,
        grid_spec=pltpu.PrefetchScalarGridSpec(
            num_scalar_prefetch=2,
            grid=(n_tiles,),
            in_specs=[
                pl.BlockSpec((EM * SUBLANES, LANES), row_map),
                pl.BlockSpec((None, None, D_MODEL, 2 * D_FF), w_map),
                pl.BlockSpec((None, None, 1, 2 * D_FF), w_map),
                pl.BlockSpec((None, None, D_FF, D_MODEL), w_map),
                pl.BlockSpec((None, None, 1, D_MODEL), w_map),
            ],
            out_specs=pl.BlockSpec((EM * SUBLANES, LANES), lambda i, te, nu: (i, 0)),
            scratch_shapes=[
                pltpu.VMEM((D_MODEL, 2 * D_FF), jnp.bfloat16),
                pltpu.VMEM((D_FF, D_MODEL), jnp.bfloat16),
            ],
        ),
        out_shape=jax.ShapeDtypeStruct(xs.shape, jnp.float32),
        compiler_params=_cparams(("arbitrary",)),
        name="moe_experts",
    )(tile_expert, n_used, xs, w_gu, b_gu.reshape(L, E, 1, 2 * D_FF), w_down,
      b_down.reshape(L, E, 1, D_MODEL))


def _combine_kernel(final, pos_ref, ys_ref, x_ref, gate_ref, vec_ref, o_ref, buf_ref, sem):
    i = pl.program_id(0)
    n = pl.num_programs(0)
    slot_rows = TOP_K * TM * SUBLANES

    def issue(tile, slot):
        def one(r, carry):
            for k in range(TOP_K):
                src = pl.multiple_of(pos_ref[k * (n * TM) + tile * TM + r] * SUBLANES, SUBLANES)
                dst = pl.multiple_of(slot * slot_rows + (k * TM + r) * SUBLANES, SUBLANES)
                pltpu.make_async_copy(ys_ref.at[pl.ds(src, SUBLANES)],
                                      buf_ref.at[pl.ds(dst, SUBLANES)], sem.at[slot]).start(priority=k % 2)
            return carry

        lax.fori_loop(0, TM, one, 0)

    @pl.when(i == 0)
    def _():
        issue(0, 0)

    @pl.when(i + 1 < n)
    def _():
        issue(i + 1, (i + 1) % 2)

    slot = i % 2
    base = pl.multiple_of(slot * slot_rows, slot_rows)
    pltpu.make_async_copy(ys_ref.at[pl.ds(0, slot_rows)], buf_ref.at[pl.ds(base, slot_rows)],
                          sem.at[slot]).wait()

    gates = gate_ref[...]
    pieces = []
    for c in range(D_CHUNKS):
        f = None
        for k in range(TOP_K):
            rows = buf_ref[pl.ds(base + k * TM * SUBLANES + c, TM, stride=SUBLANES), :]
            term = gates[:, k:k + 1] * rows
            f = term if f is None else f + term
        pieces.append(f)
    f = jnp.concatenate(pieces, axis=1)
    x_new = x_ref[...] + vec_ref[0:1, :] * f
    if final:
        x_new = _rms(x_new) * vec_ref[1:2, :]
    o_ref[...] = x_new


def _combine(pos, ys, x, gates, vecs, final):
    S = x.shape[0]
    return pl.pallas_call(
        functools.partial(_combine_kernel, final),
        grid_spec=pltpu.PrefetchScalarGridSpec(
            num_scalar_prefetch=1,
            grid=(S // TM,),
            in_specs=[
                pl.BlockSpec(memory_space=pl.ANY),
                pl.BlockSpec((TM, D_MODEL), lambda i, p: (i, 0)),
                pl.BlockSpec((TM, LANES), lambda i, p: (i, 0)),
                pl.BlockSpec((SUBLANES, D_MODEL), lambda i, p: (0, 0)),
            ],
            out_specs=pl.BlockSpec((TM, D_MODEL), lambda i, p: (i, 0)),
            scratch_shapes=[
                pltpu.VMEM((2 * TOP_K * TM * SUBLANES, LANES), jnp.float32),
                pltpu.SemaphoreType.DMA((2,)),
            ],
        ),
        out_shape=jax.ShapeDtypeStruct((S, D_MODEL), jnp.float32),
        compiler_params=_cparams(("arbitrary",)),
        name="moe_combine",
    )(pos, ys, x, gates, vecs)


def _moe(layer, x_new, hf_tiles, meta, gates, counts, w_gu, b_gu, w_down, b_down, out_vecs, final):
    S = x_new.shape[0]
    n_tiles = (S * TOP_K) // EM + N_EXPERTS
    cnt = counts[0, :N_EXPERTS]
    padded = ((cnt + EM - 1) // EM) * EM
    ends = jnp.cumsum(padded)
    offs = ends - padded
    eid = meta[:TOP_K]
    rank = meta[TOP_K:2 * TOP_K]
    base = jnp.zeros_like(eid)
    for e in range(N_EXPERTS):
        base = jnp.where(eid == e, offs[e], base)
    pos = (base + rank).reshape(-1).astype(jnp.int32)
    tile_ends = ends // EM
    tile_ids = jnp.arange(n_tiles, dtype=jnp.int32)
    tile_expert = jnp.minimum(
        jnp.sum((tile_ends[None, :] <= tile_ids[:, None]).astype(jnp.int32), axis=1),
        N_EXPERTS - 1).astype(jnp.int32)
    n_used = tile_ends[-1:].astype(jnp.int32)
    trailing = tile_ends[-1] + jnp.arange(N_EXPERTS, dtype=jnp.int32)
    ztile = jnp.concatenate([
        jnp.where(padded > 0, tile_ends - 1, -1),
        jnp.where(trailing < n_tiles, trailing, -1)]).astype(jnp.int32)
    xs = _dispatch(pos, ztile, hf_tiles, n_tiles * EM)
    ys = _experts(layer, tile_expert, n_used, xs, w_gu, b_gu, w_down, b_down)
    return _combine(pos, ys, x_new, gates, out_vecs, final)


def _pad_rows(rows):
    rows = [r.reshape(1, -1) for r in rows]
    rows += [jnp.zeros_like(rows[0])] * (SUBLANES - len(rows))
    return jnp.concatenate(rows, axis=0)


def _rope_tables(S, C):
    quarter = HEAD_DIM // 4
    inv_freq = ROPE_THETA ** (-np.arange(quarter, dtype=np.float32) / quarter)
    n_rows = S // GRID_W
    ang_r = np.arange(n_rows, dtype=np.float32)[:, None] * inv_freq[None, :]
    ang_c = np.arange(GRID_W, dtype=np.float32)[:, None] * inv_freq[None, :]

    def expand(tab_r, tab_c, sign):
        r = jnp.repeat(jnp.asarray(tab_r, jnp.float32), GRID_W, axis=0)
        cc = jnp.tile(jnp.asarray(tab_c, jnp.float32), (n_rows, 1))
        return jnp.concatenate([sign * r, r, sign * cc, cc], axis=1)

    cos = expand(np.cos(ang_r), np.cos(ang_c), 1.0)
    sin = expand(np.sin(ang_r), np.sin(ang_c), -1.0)
    cos = jnp.concatenate([cos, jnp.ones((C, HEAD_DIM), jnp.float32)], axis=0)
    sin = jnp.concatenate([sin, jnp.zeros((C, HEAD_DIM), jnp.float32)], axis=0)
    return cos, sin


def kernel(x, c, ctx, c_ctx, ada_w, ada_b, norm_mix, norm_ffn, attn_w_qkv, attn_q_norm, attn_k_norm,
           attn_w_o, pool_w, pool_scale, moe_router_w, moe_router_b, moe_w_gu, moe_b_gu, moe_w_down,
           moe_b_down, final_norm):
    B, S, D = x.shape
    C = ctx.shape[1]
    assert B == 1 and D == D_MODEL and S % GRID_W == 0
    assert S % TM == 0 and C % TM == 0 and S % RM == 0 and (S * TOP_K) % EM == 0
    x2d = x.reshape(S, D)
    ctx2d = ctx.reshape(C, D)
    bf = jnp.bfloat16

    cvec = _pad_rows([c.reshape(-1), c_ctx])
    mod = _modulation(cvec, ada_w, ada_b)
    m_l = [mod[l, 0].reshape(N_MOD, D) for l in range(2)]
    m_c0 = mod[0, 1].reshape(N_MOD, D)

    def split_router(w):
        w = jnp.pad(w, ((0, 0), (0, LANES - N_EXPERTS)))
        hi = w.astype(bf)
        lo = (w - hi.astype(jnp.float32)).astype(bf)
        return jnp.concatenate([hi, lo], axis=1)

    rw = [split_router(moe_router_w[l]) for l in range(2)]
    rb = [jnp.pad(moe_router_b[l], (0, LANES - N_EXPERTS), constant_values=NEG_BIG).reshape(1, LANES)
          for l in range(2)]

    cos_t, sin_t = _rope_tables(S, C)
    qkv_vecs = _pad_rows([m_l[0][0], m_l[0][1], m_c0[0], m_c0[1], norm_mix[0]])
    q, k, v = _qkv(x2d, ctx2d, qkv_vecs, attn_w_qkv[0].astype(bf), attn_q_norm[0].reshape(1, -1),
                   attn_k_norm[0].reshape(1, -1), cos_t, sin_t)
    o = _attention(q, k, v, S)
    vecs0 = _pad_rows([m_l[0][2], m_l[0][3], m_l[0][4], norm_ffn[0]])
    x1, hf, meta, gates, counts = _post_attn(x2d, o, attn_w_o[0].astype(bf), vecs0, rw[0], rb[0])
    x2 = _moe(0, x1, hf, meta, gates, counts, moe_w_gu, moe_b_gu, moe_w_down, moe_b_down,
              _pad_rows([m_l[0][5]]), final=False)

    mvecs = _pad_rows([m_l[1][0], m_l[1][1], norm_mix[1], pool_scale[0]])
    vecs1 = _pad_rows([m_l[1][2], m_l[1][3], m_l[1][4], norm_ffn[1]])
    x3, hf, meta, gates, counts = _pool(x2, mvecs, pool_w[0].astype(bf), vecs1, rw[1], rb[1])
    out = _moe(1, x3, hf, meta, gates, counts, moe_w_gu, moe_b_gu, moe_w_down, moe_b_down,
               _pad_rows([m_l[1][5], final_norm]), final=True)
    return out.reshape(B, S, D)
```

```python
import functools

import numpy as np
import jax
import jax.numpy as jnp
from jax import lax
from jax.experimental import pallas as pl
from jax.experimental.pallas import tpu as pltpu

D_MODEL = 1024
GRID_W = 64
N_HEADS = 8
N_KV_HEADS = 2
KV_GROUP = N_HEADS // N_KV_HEADS
HEAD_DIM = 128
ROPE_THETA = 10000.0
POOL_WINDOWS = (2, 4, 8, 16)
POOL_GROUP = D_MODEL // len(POOL_WINDOWS)
POOL_HALO = 8
N_EXPERTS = 32
TOP_K = 4
D_FF = D_MODEL
SWIGLU_LIMIT = 7.0
SWIGLU_ALPHA = 1.702
NORM_EPS = 1e-6
N_MOD = 6

LANES = 128
SUBLANES = 8
D_CHUNKS = D_MODEL // LANES
TM = 256
EM = 512
RM = 512
AQ = 512
TK_MAX = 1280
ATTN_ROW_SUM_FLOOR = 2.0 ** -100
ATTN_UNROLL = 2
VMEM_LIMIT = 48 * 1024 * 1024
ATTN_VMEM_LIMIT = 56 * 1024 * 1024
LOG2E = 1.4426950408889634
NEG_BIG = -1e30

_HI = lax.Precision.HIGHEST


def _cparams(sem):
    return pltpu.CompilerParams(dimension_semantics=sem, vmem_limit_bytes=VMEM_LIMIT)


def _rms(x, eps=NORM_EPS):
    return x * lax.rsqrt(jnp.mean(x * x, axis=-1, keepdims=True) + eps)


def _mod_kernel(cv_ref, w_ref, b_ref, o_ref):
    cv = cv_ref[...]
    s = cv / (1.0 + jnp.exp(-cv))
    o_ref[0] = jnp.dot(s, w_ref[0], precision=_HI, preferred_element_type=jnp.float32) + b_ref[0]


def _modulation(cvec, ada_w, ada_b):
    depth = ada_w.shape[0]
    nblk = 4
    bw = N_MOD * D_MODEL // nblk
    return pl.pallas_call(
        _mod_kernel,
        grid=(depth, nblk),
        in_specs=[
            pl.BlockSpec((SUBLANES, D_MODEL), lambda l, j: (0, 0)),
            pl.BlockSpec((1, D_MODEL, bw), lambda l, j: (l, 0, j)),
            pl.BlockSpec((1, 1, bw), lambda l, j: (l, 0, j)),
        ],
        out_specs=pl.BlockSpec((1, SUBLANES, bw), lambda l, j: (l, 0, j)),
        out_shape=jax.ShapeDtypeStruct((depth, SUBLANES, N_MOD * D_MODEL), jnp.float32),
        compiler_params=_cparams(("arbitrary", "arbitrary")),
        name="modulation",
    )(cvec, ada_w, ada_b.reshape(depth, 1, N_MOD * D_MODEL))


def _rope(x, cos, sin_signed):
    lane = lax.broadcasted_iota(jnp.int32, x.shape, 1)
    partner = jnp.where((lane % 64) < 32, pltpu.roll(x, LANES - 32, 1), pltpu.roll(x, 32, 1))
    return x * cos + partner * sin_signed


def _qkv_kernel(n_lat_tiles, x_ref, ctx_ref, vec_ref, w_ref, qg_ref, kg_ref, cos_ref, sin_ref,
                q_ref, k_ref, v_ref):
    i = pl.program_id(0)
    is_ctx = i >= n_lat_tiles
    xt = jnp.where(is_ctx, ctx_ref[...], x_ref[...])
    sh = jnp.where(is_ctx, vec_ref[2:3, :], vec_ref[0:1, :])
    sc = jnp.where(is_ctx, vec_ref[3:4, :], vec_ref[1:2, :])
    h = _rms(xt) * vec_ref[4:5, :] * (1.0 + sc) + sh
    qkv = jnp.dot(h.astype(jnp.bfloat16), w_ref[...], preferred_element_type=jnp.float32)
    cos = cos_ref[...]
    sin = sin_ref[...]
    q_scale = (HEAD_DIM ** -0.5) * LOG2E
    for hd in range(N_HEADS):
        qh = _rms(qkv[:, hd * HEAD_DIM:(hd + 1) * HEAD_DIM]) * qg_ref[...]
        q_ref[hd] = (_rope(qh, cos, sin) * q_scale).astype(jnp.bfloat16)
    for kv in range(N_KV_HEADS):
        c0 = (N_HEADS + kv) * HEAD_DIM
        kh = _rms(qkv[:, c0:c0 + HEAD_DIM]) * kg_ref[...]
        k_ref[kv] = _rope(kh, cos, sin).astype(jnp.bfloat16)
        c1 = (N_HEADS + N_KV_HEADS + kv) * HEAD_DIM
        v_ref[kv] = qkv[:, c1:c1 + HEAD_DIM].astype(jnp.bfloat16)


def _qkv(x, ctx, vecs, w_qkv, q_g, k_g, cos_t, sin_t):
    S, C = x.shape[0], ctx.shape[0]
    n_lat = S // TM
    n_ctx = C // TM
    T = S + C
    last = n_lat - 1
    qkv_w = w_qkv.shape[1]
    out_shapes = (
        jax.ShapeDtypeStruct((N_HEADS, T, HEAD_DIM), jnp.bfloat16),
        jax.ShapeDtypeStruct((N_KV_HEADS, T, HEAD_DIM), jnp.bfloat16),
        jax.ShapeDtypeStruct((N_KV_HEADS, T, HEAD_DIM), jnp.bfloat16),
    )
    return pl.pallas_call(
        functools.partial(_qkv_kernel, n_lat),
        grid=(n_lat + n_ctx,),
        in_specs=[
            pl.BlockSpec((TM, D_MODEL), lambda i: (jnp.minimum(i, last), 0)),
            pl.BlockSpec((TM, D_MODEL), lambda i: (jnp.maximum(i - n_lat, 0), 0)),
            pl.BlockSpec((SUBLANES, D_MODEL), lambda i: (0, 0)),
            pl.BlockSpec((D_MODEL, qkv_w), lambda i: (0, 0)),
            pl.BlockSpec((1, HEAD_DIM), lambda i: (0, 0)),
            pl.BlockSpec((1, HEAD_DIM), lambda i: (0, 0)),
            pl.BlockSpec((TM, HEAD_DIM), lambda i: (i, 0)),
            pl.BlockSpec((TM, HEAD_DIM), lambda i: (i, 0)),
        ],
        out_specs=(
            pl.BlockSpec((N_HEADS, TM, HEAD_DIM), lambda i: (0, i, 0)),
            pl.BlockSpec((N_KV_HEADS, TM, HEAD_DIM), lambda i: (0, i, 0)),
            pl.BlockSpec((N_KV_HEADS, TM, HEAD_DIM), lambda i: (0, i, 0)),
        ),
        out_shape=out_shapes,
        compiler_params=_cparams(("arbitrary",)),
        name="qkv_proj",
    )(x, ctx, vecs, w_qkv, q_g, k_g, cos_t, sin_t)


def _lane_repeat(x, n):
    return jnp.concatenate([x] * n, axis=1)


def _attn_kernel(n_chunks, tk, q_ref, k_ref, v_ref, o_ref, s_ref, p_ref, a_ref, mc_ref, m_ref, acc_ref):
    rows = KV_GROUP * TM
    ones = jnp.ones((tk, HEAD_DIM), jnp.bfloat16)

    def scores(c, slot):
        k = k_ref[0, pl.ds(pl.multiple_of(c * tk, tk), tk), :]
        q = q_ref[...].reshape(rows, HEAD_DIM)
        s = lax.dot_general(q, k, (((1,), (1,)), ((), ())), preferred_element_type=jnp.float32)
        s_ref[slot] = s
        mc_ref[slot] = jnp.broadcast_to(jnp.max(s, axis=1, keepdims=True), (rows, LANES))

    def softmax(slot):
        m_old = m_ref[...]
        m_new = jnp.maximum(m_old, mc_ref[slot])
        a_ref[slot] = jnp.exp2(m_old - m_new)
        m_ref[...] = m_new
        p_ref[slot] = jnp.exp2(s_ref[slot] - _lane_repeat(m_new, tk // LANES)).astype(jnp.bfloat16)

    def values(c, slot):
        v = v_ref[0, pl.ds(pl.multiple_of(c * tk, tk), tk), :]
        vx = jnp.concatenate([v, ones], axis=1)
        acc_ref[...] = (acc_ref[...] * _lane_repeat(a_ref[slot], 2)
                        + jnp.dot(p_ref[slot], vx, preferred_element_type=jnp.float32))

    def iteration(t, par):
        scores(t + 1, 1 - par)
        softmax(par)
        values(jnp.maximum(t - 1, 0), 1 - par)

    m_ref[...] = jnp.full(m_ref.shape, -jnp.inf, jnp.float32)
    acc_ref[...] = jnp.zeros(acc_ref.shape, jnp.float32)
    p_ref[1] = jnp.zeros(p_ref.shape[1:], jnp.bfloat16)
    a_ref[1] = jnp.ones(a_ref.shape[1:], jnp.float32)
    scores(0, 0)

    n_steady = n_chunks - 1

    def group(u, carry):
        for j in range(ATTN_UNROLL):
            iteration(ATTN_UNROLL * u + j, j % 2)
        return carry

    n_groups = n_steady // ATTN_UNROLL
    lax.fori_loop(0, n_groups, group, 0)
    for t in range(n_groups * ATTN_UNROLL, n_steady):
        iteration(t, t % 2)
    last = n_chunks - 1
    softmax(last % 2)
    values(max(last - 1, 0), 1 - last % 2)
    values(last, last % 2)

    acc = acc_ref[...]
    o = acc[:, :HEAD_DIM] / acc[:, HEAD_DIM:]
    for g in range(KV_GROUP):
        o_ref[:, g * HEAD_DIM:(g + 1) * HEAD_DIM] = o[g * TM:(g + 1) * TM].astype(o_ref.dtype)


def _key_chunk(T):
    return max(c for c in range(LANES, TK_MAX + 1, LANES) if T % c == 0)


def _attn_bound_kernel(n_chunks, tk, q_ref, k_ref, v_ref, o_ref, lmin_ref, p_ref, m_ref, acc_ref, kmax_ref):
    i = pl.program_id(1)
    rows = KV_GROUP * AQ
    ones = jnp.ones((tk, HEAD_DIM), jnp.bfloat16)

    @pl.when(i == 0)
    def _():
        def body(c, mx):
            k = k_ref[0, pl.ds(pl.multiple_of(c * tk, tk), tk), :].astype(jnp.float32)
            n2 = jnp.sum(k * k, axis=1, keepdims=True)
            return jnp.maximum(mx, jnp.max(n2, axis=0, keepdims=True))

        kmax2 = lax.fori_loop(0, n_chunks, body, jnp.zeros((1, 1), jnp.float32))
        kmax_ref[...] = jnp.broadcast_to(kmax2, kmax_ref.shape)

    qf = q_ref[...].reshape(rows, HEAD_DIM).astype(jnp.float32)
    qn2 = jnp.sum(qf * qf, axis=1, keepdims=True)
    m_ref[...] = jnp.broadcast_to(jnp.sqrt(qn2 * kmax_ref[0:1, 0:1]), (rows, LANES))
    acc_ref[...] = jnp.zeros(acc_ref.shape, jnp.float32)

    def probs(c, slot):
        k = k_ref[0, pl.ds(pl.multiple_of(c * tk, tk), tk), :]
        q = q_ref[...].reshape(rows, HEAD_DIM)
        s = lax.dot_general(q, k, (((1,), (1,)), ((), ())), preferred_element_type=jnp.float32)
        p_ref[slot] = jnp.exp2(s - _lane_repeat(m_ref[...], tk // LANES)).astype(jnp.bfloat16)

    def values(c, slot):
        v = v_ref[0, pl.ds(pl.multiple_of(c * tk, tk), tk), :]
        vx = jnp.concatenate([v, ones], axis=1)
        acc_ref[...] += jnp.dot(p_ref[slot], vx, preferred_element_type=jnp.float32)

    def iteration(t, par):
        probs(t + 1, 1 - par)
        values(t, par)

    probs(0, 0)
    n_steady = n_chunks - 1

    def group(u, carry):
        for j in range(ATTN_UNROLL):
            iteration(ATTN_UNROLL * u + j, j % 2)
        return carry

    n_groups = n_steady // ATTN_UNROLL
    lax.fori_loop(0, n_groups, group, 0)
    for t in range(n_groups * ATTN_UNROLL, n_steady):
        iteration(t, t % 2)
    values(n_chunks - 1, (n_chunks - 1) % 2)

    acc = acc_ref[...]
    l = acc[:, HEAD_DIM:]
    o = acc[:, :HEAD_DIM] / l
    for g in range(KV_GROUP):
        o_ref[:, g * HEAD_DIM:(g + 1) * HEAD_DIM] = o[g * AQ:(g + 1) * AQ].astype(o_ref.dtype)
    lmin_ref[0] = jnp.broadcast_to(jnp.min(l, axis=0, keepdims=True), (SUBLANES, LANES))


def _attention_bound(q, k, v, S):
    T = k.shape[1]
    tk = _key_chunk(T)
    rows = KV_GROUP * AQ
    n_q = S // AQ
    return pl.pallas_call(
        functools.partial(_attn_bound_kernel, T // tk, tk),
        grid=(N_KV_HEADS, n_q),
        in_specs=[
            pl.BlockSpec((KV_GROUP, AQ, HEAD_DIM), lambda g, i: (g, i, 0)),
            pl.BlockSpec((1, T, HEAD_DIM), lambda g, i: (g, 0, 0)),
            pl.BlockSpec((1, T, HEAD_DIM), lambda g, i: (g, 0, 0)),
        ],
        out_specs=(
            pl.BlockSpec((AQ, KV_GROUP * HEAD_DIM), lambda g, i: (i, g)),
            pl.BlockSpec((1, SUBLANES, LANES), lambda g, i: (g * n_q + i, 0, 0)),
        ),
        out_shape=(
            jax.ShapeDtypeStruct((S, N_HEADS * HEAD_DIM), jnp.bfloat16),
            jax.ShapeDtypeStruct((N_KV_HEADS * n_q, SUBLANES, LANES), jnp.float32),
        ),
        scratch_shapes=[
            pltpu.VMEM((2, rows, tk), jnp.bfloat16),
            pltpu.VMEM((rows, LANES), jnp.float32),
            pltpu.VMEM((rows, 2 * HEAD_DIM), jnp.float32),
            pltpu.VMEM((SUBLANES, LANES), jnp.float32),
        ],
        compiler_params=pltpu.CompilerParams(dimension_semantics=("arbitrary", "arbitrary"),
                                             vmem_limit_bytes=ATTN_VMEM_LIMIT),
        name="flash_attention_bound",
    )(q, k, v)


def _attention(q, k, v, S):
    o, lmin = _attention_bound(q, k, v, S)
    ok = jnp.min(lmin) >= ATTN_ROW_SUM_FLOOR
    return lax.cond(ok, lambda: o, lambda: _attention_running_max(q, k, v, S))


def _attention_running_max(q, k, v, S):
    T = k.shape[1]
    tk = _key_chunk(T)
    rows = KV_GROUP * TM
    return pl.pallas_call(
        functools.partial(_attn_kernel, T // tk, tk),
        grid=(N_KV_HEADS, S // TM),
        in_specs=[
            pl.BlockSpec((KV_GROUP, TM, HEAD_DIM), lambda g, i: (g, i, 0)),
            pl.BlockSpec((1, T, HEAD_DIM), lambda g, i: (g, 0, 0)),
            pl.BlockSpec((1, T, HEAD_DIM), lambda g, i: (g, 0, 0)),
        ],
        out_specs=pl.BlockSpec((TM, KV_GROUP * HEAD_DIM), lambda g, i: (i, g)),
        out_shape=jax.ShapeDtypeStruct((S, N_HEADS * HEAD_DIM), jnp.bfloat16),
        scratch_shapes=[
            pltpu.VMEM((2, rows, tk), jnp.float32),
            pltpu.VMEM((2, rows, tk), jnp.bfloat16),
            pltpu.VMEM((2, rows, LANES), jnp.float32),
            pltpu.VMEM((2, rows, LANES), jnp.float32),
            pltpu.VMEM((rows, LANES), jnp.float32),
            pltpu.VMEM((rows, 2 * HEAD_DIM), jnp.float32),
        ],
        compiler_params=pltpu.CompilerParams(dimension_semantics=("arbitrary", "arbitrary"),
                                             vmem_limit_bytes=ATTN_VMEM_LIMIT),
        name="flash_attention",
    )(q, k, v)


def _router_out_shapes(S):
    return (
        jax.ShapeDtypeStruct((S, D_MODEL), jnp.float32),
        jax.ShapeDtypeStruct((S * SUBLANES, LANES), jnp.float32),
        jax.ShapeDtypeStruct((SUBLANES, S), jnp.int32),
        jax.ShapeDtypeStruct((S, LANES), jnp.float32),
        jax.ShapeDtypeStruct((SUBLANES, LANES), jnp.int32),
    )


def _router_out_specs():
    return (
        pl.BlockSpec((RM, D_MODEL), lambda i: (i, 0)),
        pl.BlockSpec((RM * SUBLANES, LANES), lambda i: (i, 0)),
        pl.BlockSpec((SUBLANES, RM), lambda i: (0, i)),
        pl.BlockSpec((RM, LANES), lambda i: (i, 0)),
        pl.BlockSpec((SUBLANES, LANES), lambda i: (0, 0)),
    )


def _router_core(x_new, vec_ref, rw_ref, rb_ref, cnt_ref, x_out_ref, hf_ref, meta_ref, gate_ref,
                 counts_ref):
    i = pl.program_id(0)

    @pl.when(i == 0)
    def _():
        cnt_ref[...] = jnp.zeros_like(cnt_ref)

    x_out_ref[...] = x_new
    hf = _rms(x_new) * vec_ref[3:4, :] * (1.0 + vec_ref[2:3, :]) + vec_ref[1:2, :]
    for c in range(D_CHUNKS):
        hf_ref[pl.ds(c, RM, stride=SUBLANES), :] = hf[:, c * LANES:(c + 1) * LANES]

    h1 = hf.astype(jnp.bfloat16)
    r1 = hf - h1.astype(jnp.float32)
    h2 = r1.astype(jnp.bfloat16)
    h3 = (r1 - h2.astype(jnp.float32)).astype(jnp.bfloat16)
    w12 = rw_ref[:, :2 * LANES]
    cross = (jnp.dot(h1, w12, preferred_element_type=jnp.float32)
             + jnp.dot(h2, w12, preferred_element_type=jnp.float32))
    small = (jnp.dot(h3, rw_ref[:, :LANES], preferred_element_type=jnp.float32)
             + jnp.dot(h1, rw_ref[:, 2 * LANES:], preferred_element_type=jnp.float32))
    logits = cross[:, :LANES] + cross[:, LANES:] + small + rb_ref[...]
    lane = lax.broadcasted_iota(jnp.int32, logits.shape, 1)
    lane_f = lane.astype(jnp.float32)
    vals, idxs, hots = [], [], []
    work = logits
    for _ in range(TOP_K):
        mx = jnp.max(work, axis=1, keepdims=True)
        idx = jnp.min(jnp.where(work == mx, lane_f, float(LANES)), axis=1, keepdims=True)
        hot = lane_f == idx
        work = jnp.where(hot, -jnp.inf, work)
        vals.append(mx)
        idxs.append(idx.astype(jnp.int32))
        hots.append(hot)
    exps = [jnp.exp(vk - vals[0]) for vk in vals]
    denom = exps[0] + exps[1] + exps[2] + exps[3]

    any_hot = hots[0] | hots[1] | hots[2] | hots[3]
    hot_f = jnp.where(any_hot, 1.0, 0.0)
    r_i = lax.broadcasted_iota(jnp.int32, (RM, RM), 0)
    c_i = lax.broadcasted_iota(jnp.int32, (RM, RM), 1)
    lower = jnp.where(c_i < r_i, 1.0, 0.0).astype(jnp.bfloat16)
    before = jnp.dot(lower, hot_f.astype(jnp.bfloat16), preferred_element_type=jnp.float32) + cnt_ref[0:1, :]
    cnt_ref[0:1, :] = cnt_ref[0:1, :] + jnp.sum(hot_f, axis=0, keepdims=True)

    meta = jnp.zeros(logits.shape, jnp.float32)
    gates = jnp.zeros(logits.shape, jnp.float32)
    for k in range(TOP_K):
        rank = jnp.sum(jnp.where(hots[k], before, 0.0), axis=1, keepdims=True)
        meta = jnp.where(lane == k, idxs[k].astype(jnp.float32), meta)
        meta = jnp.where(lane == TOP_K + k, rank, meta)
        gates = jnp.where(lane == k, exps[k] / denom, gates)
    meta_ref[...] = meta.T[:SUBLANES, :].astype(jnp.int32)
    gate_ref[...] = gates
    counts_ref[...] = jnp.broadcast_to(cnt_ref[0:1, :], counts_ref.shape).astype(jnp.int32)


def _post_attn_kernel(x_ref, o_ref, wo_ref, vec_ref, rw_ref, rb_ref,
                      x_out_ref, hf_ref, meta_ref, gate_ref, counts_ref, cnt_ref):
    y = jnp.dot(o_ref[...], wo_ref[...], preferred_element_type=jnp.float32)
    x_new = x_ref[...] + vec_ref[0:1, :] * y
    _router_core(x_new, vec_ref, rw_ref, rb_ref, cnt_ref, x_out_ref, hf_ref, meta_ref, gate_ref,
                 counts_ref)


def _post_attn(x, o, w_o, vecs, rw, rb):
    S = x.shape[0]
    return pl.pallas_call(
        _post_attn_kernel,
        grid=(S // RM,),
        in_specs=[
            pl.BlockSpec((RM, D_MODEL), lambda i: (i, 0)),
            pl.BlockSpec((RM, D_MODEL), lambda i: (i, 0)),
            pl.BlockSpec((D_MODEL, D_MODEL), lambda i: (0, 0)),
            pl.BlockSpec((SUBLANES, D_MODEL), lambda i: (0, 0)),
            pl.BlockSpec((D_MODEL, 3 * LANES), lambda i: (0, 0)),
            pl.BlockSpec((1, LANES), lambda i: (0, 0)),
        ],
        out_specs=_router_out_specs(),
        out_shape=_router_out_shapes(S),
        scratch_shapes=[pltpu.VMEM((SUBLANES, LANES), jnp.float32)],
        compiler_params=_cparams(("arbitrary",)),
        name="post_attn_router",
    )(x, o, w_o, vecs, rw, rb)


def _pool_kernel(S, x_ref, prev_ref, next_ref, mvec_ref, pw_ref, vec_ref, rw_ref, rb_ref,
                 x_out_ref, hf_ref, meta_ref, gate_ref, counts_ref, cnt_ref, buf_ref):
    i = pl.program_id(0)
    n = pl.num_programs(0)

    def hmod(xx):
        return _rms(xx) * mvec_ref[2:3, :] * (1.0 + mvec_ref[1:2, :]) + mvec_ref[0:1, :]

    x_cur = x_ref[...]
    h_cur = hmod(x_cur)
    buf_ref[0:POOL_HALO, :] = jnp.where(i > 0, hmod(prev_ref[...]), 0.0)
    buf_ref[POOL_HALO:POOL_HALO + RM, :] = h_cur
    buf_ref[POOL_HALO + RM:, :] = jnp.where(i < n - 1, hmod(next_ref[...]), 0.0)

    t = i * RM + lax.broadcasted_iota(jnp.int32, (RM, 1), 0)
    ys = []
    for g, w in enumerate(POOL_WINDOWS):
        c0 = g * POOL_GROUP
        acc = None
        for d in range(-(w // 2), w - w // 2):
            piece = buf_ref[POOL_HALO + d:POOL_HALO + d + RM, c0:c0 + POOL_GROUP]
            acc = piece if acc is None else acc + piece
        cnt = jnp.minimum(t - w // 2 + w, S) - jnp.maximum(t - w // 2, 0)
        pooled = acc / cnt.astype(jnp.float32) - h_cur[:, c0:c0 + POOL_GROUP]
        ys.append(jnp.dot(pooled.astype(jnp.bfloat16), pw_ref[g], preferred_element_type=jnp.float32))
    y = jnp.concatenate(ys, axis=1) * mvec_ref[3:4, :]
    x_new = x_cur + vec_ref[0:1, :] * y
    _router_core(x_new, vec_ref, rw_ref, rb_ref, cnt_ref, x_out_ref, hf_ref, meta_ref, gate_ref,
                 counts_ref)


def _pool(x, mvecs, pool_w, vecs, rw, rb):
    S = x.shape[0]
    hb = RM // POOL_HALO
    n_halo = S // POOL_HALO
    return pl.pallas_call(
        functools.partial(_pool_kernel, S),
        grid=(S // RM,),
        in_specs=[
            pl.BlockSpec((RM, D_MODEL), lambda i: (i, 0)),
            pl.BlockSpec((POOL_HALO, D_MODEL), lambda i: (jnp.maximum(i * hb - 1, 0), 0)),
            pl.BlockSpec((POOL_HALO, D_MODEL), lambda i: (jnp.minimum((i + 1) * hb, n_halo - 1), 0)),
            pl.BlockSpec((SUBLANES, D_MODEL), lambda i: (0, 0)),
            pl.BlockSpec((len(POOL_WINDOWS), POOL_GROUP, POOL_GROUP), lambda i: (0, 0, 0)),
            pl.BlockSpec((SUBLANES, D_MODEL), lambda i: (0, 0)),
            pl.BlockSpec((D_MODEL, 3 * LANES), lambda i: (0, 0)),
            pl.BlockSpec((1, LANES), lambda i: (0, 0)),
        ],
        out_specs=_router_out_specs(),
        out_shape=_router_out_shapes(S),
        scratch_shapes=[
            pltpu.VMEM((SUBLANES, LANES), jnp.float32),
            pltpu.VMEM((RM + 2 * POOL_HALO, D_MODEL), jnp.float32),
        ],
        compiler_params=_cparams(("arbitrary",)),
        name="pool_router",
    )(x, x, x, mvecs, pool_w, vecs, rw, rb)


def _dispatch_kernel(pos_ref, ztile_ref, hf_ref, xs_ref, stage_ref, zeros_ref, load_sem, row_sem, zero_sem):
    i = pl.program_id(0)
    n = pl.num_programs(0)
    tile_rows = TM * SUBLANES
    pairs = TM * TOP_K

    def load(tile, slot):
        src = pl.multiple_of(tile * tile_rows, tile_rows)
        dst = pl.multiple_of(slot * tile_rows, tile_rows)
        return pltpu.make_async_copy(hf_ref.at[pl.ds(src, tile_rows)],
                                     stage_ref.at[pl.ds(dst, tile_rows)], load_sem.at[slot])

    def rows_wait():
        pltpu.make_async_copy(xs_ref.at[pl.ds(0, pairs * SUBLANES)],
                              xs_ref.at[pl.ds(0, pairs * SUBLANES)], row_sem).wait()

    def zero_copy(e):
        zero_rows = EM * SUBLANES
        start = pl.multiple_of(jnp.maximum(ztile_ref[e], 0) * zero_rows, zero_rows)
        return pltpu.make_async_copy(zeros_ref, xs_ref.at[pl.ds(start, zero_rows)], zero_sem)

    @pl.when(i == 0)
    def _():
        zeros_ref[...] = jnp.zeros_like(zeros_ref)
        load(0, 0).start()

        def zero_start(e, carry):
            @pl.when(ztile_ref[e] >= 0)
            def _():
                zero_copy(e).start()
            return carry

        def zero_wait(e, carry):
            @pl.when(ztile_ref[e] >= 0)
            def _():
                zero_copy(e).wait()
            return carry

        lax.fori_loop(0, 2 * N_EXPERTS, zero_start, 0)
        lax.fori_loop(0, 2 * N_EXPERTS, zero_wait, 0)

    @pl.when(i > 0)
    def _():
        rows_wait()

    @pl.when(i + 1 < n)
    def _():
        load(i + 1, (i + 1) % 2).start()

    slot = i % 2
    load(i, slot).wait()

    def issue(r, carry):
        src = pl.multiple_of(slot * tile_rows + r * SUBLANES, SUBLANES)
        for k in range(TOP_K):
            dst = pl.multiple_of(pos_ref[k * (n * TM) + i * TM + r] * SUBLANES, SUBLANES)
            pltpu.make_async_copy(stage_ref.at[pl.ds(src, SUBLANES)],
                                  xs_ref.at[pl.ds(dst, SUBLANES)], row_sem).start(priority=k % 2)
        return carry

    lax.fori_loop(0, TM, issue, 0)

    @pl.when(i == n - 1)
    def _():
        rows_wait()


def _dispatch(pos, ztile, hf_tiles, n_rows):
    S = hf_tiles.shape[0] // SUBLANES
    tile_rows = TM * SUBLANES
    return pl.pallas_call(
        _dispatch_kernel,
        grid_spec=pltpu.PrefetchScalarGridSpec(
            num_scalar_prefetch=2,
            grid=(S // TM,),
            in_specs=[pl.BlockSpec(memory_space=pl.ANY)],
            out_specs=pl.BlockSpec(memory_space=pl.ANY),
            scratch_shapes=[
                pltpu.VMEM((2 * tile_rows, LANES), jnp.float32),
                pltpu.VMEM((EM * SUBLANES, LANES), jnp.float32),
                pltpu.SemaphoreType.DMA((2,)),
                pltpu.SemaphoreType.DMA(()),
                pltpu.SemaphoreType.DMA(()),
            ],
        ),
        out_shape=jax.ShapeDtypeStruct((n_rows * SUBLANES, LANES), jnp.float32),
        compiler_params=_cparams(("arbitrary",)),
        name="moe_dispatch",
    )(pos, ztile, hf_tiles)


W_CAST_ROWS = 64


def _expert_kernel(te_ref, nused_ref, xs_ref, wgu_ref, bgu_ref, wd_ref, bd_ref, ys_ref,
                   wgu_bf_ref, wd_bf_ref):
    i = pl.program_id(0)
    used = i < nused_ref[0]
    new_expert = (i == 0) | (te_ref[i] != te_ref[jnp.maximum(i - 1, 0)])

    @pl.when(used & new_expert)
    def _():
        def cast_gu(r, carry):
            rows = pl.ds(pl.multiple_of(r * W_CAST_ROWS, W_CAST_ROWS), W_CAST_ROWS)
            wgu_bf_ref[rows, :] = wgu_ref[rows, :].astype(jnp.bfloat16)
            return carry

        def cast_d(r, carry):
            rows = pl.ds(pl.multiple_of(r * W_CAST_ROWS, W_CAST_ROWS), W_CAST_ROWS)
            wd_bf_ref[rows, :] = wd_ref[rows, :].astype(jnp.bfloat16)
            return carry

        lax.fori_loop(0, D_MODEL // W_CAST_ROWS, cast_gu, 0)
        lax.fori_loop(0, D_FF // W_CAST_ROWS, cast_d, 0)

    @pl.when(used)
    def _():
        cols = [xs_ref[pl.ds(c, EM, stride=SUBLANES), :] for c in range(D_CHUNKS)]
        x = jnp.concatenate(cols, axis=1).astype(jnp.bfloat16)
        gu = jnp.dot(x, wgu_bf_ref[...], preferred_element_type=jnp.float32) + bgu_ref[...]
        gate = jnp.minimum(gu[:, :D_FF], SWIGLU_LIMIT)
        up = jnp.clip(gu[:, D_FF:], -SWIGLU_LIMIT, SWIGLU_LIMIT)
        glu = gate / (1.0 + jnp.exp(-SWIGLU_ALPHA * gate))
        a = ((up + 1.0) * glu).astype(jnp.bfloat16)
        y = jnp.dot(a, wd_bf_ref[...], preferred_element_type=jnp.float32) + bd_ref[...]
        for c in range(D_CHUNKS):
            ys_ref[pl.ds(c, EM, stride=SUBLANES), :] = y[:, c * LANES:(c + 1) * LANES]

    @pl.when(jnp.logical_not(used))
    def _():
        ys_ref[...] = jnp.zeros_like(ys_ref)


def _experts(layer, tile_expert, n_used, xs, w_gu, b_gu, w_down, b_down):
    n_tiles = tile_expert.shape[0]
    L, E = w_gu.shape[:2]

    def row_map(i, te, nu):
        return (jnp.minimum(i, nu[0] - 1), 0)

    def w_map(i, te, nu):
        return (layer, te[i], 0, 0)

    return pl.pallas_call(
        _expert_kernel,
        grid_spec=pltpu.PrefetchScalarGridSpec(
            num_scalar_prefetch=2,
            grid=(n_tiles,),
            in_specs=[
                pl.BlockSpec((EM * SUBLANES, LANES), row_map),
                pl.BlockSpec((None, None, D_MODEL, 2 * D_FF), w_map),
                pl.BlockSpec((None, None, 1, 2 * D_FF), w_map),
                pl.BlockSpec((None, None, D_FF, D_MODEL), w_map),
                pl.BlockSpec((None, None, 1, D_MODEL), w_map),
            ],
            out_specs=pl.BlockSpec((EM * SUBLANES, LANES), lambda i, te, nu: (i, 0)),
            scratch_shapes=[
                pltpu.VMEM((D_MODEL, 2 * D_FF), jnp.bfloat16),
                pltpu.VMEM((D_FF, D_MODEL), jnp.bfloat16),
            ],
        ),
        out_shape=jax.ShapeDtypeStruct(xs.shape, jnp.float32),
        compiler_params=_cparams(("arbitrary",)),
        name="moe_experts",
    )(tile_expert, n_used, xs, w_gu, b_gu.reshape(L, E, 1, 2 * D_FF), w_down,
      b_down.reshape(L, E, 1, D_MODEL))


def _combine_kernel(final, pos_ref, ys_ref, x_ref, gate_ref, vec_ref, o_ref, buf_ref, sem):
    i = pl.program_id(0)
    n = pl.num_programs(0)
    slot_rows = TOP_K * TM * SUBLANES

    def issue(tile, slot):
        def one(r, carry):
            for k in range(TOP_K):
                src = pl.multiple_of(pos_ref[k * (n * TM) + tile * TM + r] * SUBLANES, SUBLANES)
                dst = pl.multiple_of(slot * slot_rows + (k * TM + r) * SUBLANES, SUBLANES)
                pltpu.make_async_copy(ys_ref.at[pl.ds(src, SUBLANES)],
                                      buf_ref.at[pl.ds(dst, SUBLANES)], sem.at[slot]).start(priority=k % 2)
            return carry

        lax.fori_loop(0, TM, one, 0)

    @pl.when(i == 0)
    def _():
        issue(0, 0)

    @pl.when(i + 1 < n)
    def _():
        issue(i + 1, (i + 1) % 2)

    slot = i % 2
    base = pl.multiple_of(slot * slot_rows, slot_rows)
    pltpu.make_async_copy(ys_ref.at[pl.ds(0, slot_rows)], buf_ref.at[pl.ds(base, slot_rows)],
                          sem.at[slot]).wait()

    gates = gate_ref[...]
    pieces = []
    for c in range(D_CHUNKS):
        f = None
        for k in range(TOP_K):
            rows = buf_ref[pl.ds(base + k * TM * SUBLANES + c, TM, stride=SUBLANES), :]
            term = gates[:, k:k + 1] * rows
            f = term if f is None else f + term
        pieces.append(f)
    f = jnp.concatenate(pieces, axis=1)
    x_new = x_ref[...] + vec_ref[0:1, :] * f
    if final:
        x_new = _rms(x_new) * vec_ref[1:2, :]
    o_ref[...] = x_new


def _combine(pos, ys, x, gates, vecs, final):
    S = x.shape[0]
    return pl.pallas_call(
        functools.partial(_combine_kernel, final),
        grid_spec=pltpu.PrefetchScalarGridSpec(
            num_scalar_prefetch=1,
            grid=(S // TM,),
            in_specs=[
                pl.BlockSpec(memory_space=pl.ANY),
                pl.BlockSpec((TM, D_MODEL), lambda i, p: (i, 0)),
                pl.BlockSpec((TM, LANES), lambda i, p: (i, 0)),
                pl.BlockSpec((SUBLANES, D_MODEL), lambda i, p: (0, 0)),
            ],
            out_specs=pl.BlockSpec((TM, D_MODEL), lambda i, p: (i, 0)),
            scratch_shapes=[
                pltpu.VMEM((2 * TOP_K * TM * SUBLANES, LANES), jnp.float32),
                pltpu.SemaphoreType.DMA((2,)),
            ],
        ),
        out_shape=jax.ShapeDtypeStruct((S, D_MODEL), jnp.float32),
        compiler_params=_cparams(("arbitrary",)),
        name="moe_combine",
    )(pos, ys, x, gates, vecs)


def _moe(layer, x_new, hf_tiles, meta, gates, counts, w_gu, b_gu, w_down, b_down, out_vecs, final):
    S = x_new.shape[0]
    n_tiles = (S * TOP_K) // EM + N_EXPERTS
    cnt = counts[0, :N_EXPERTS]
    padded = ((cnt + EM - 1) // EM) * EM
    ends = jnp.cumsum(padded)
    offs = ends - padded
    eid = meta[:TOP_K]
    rank = meta[TOP_K:2 * TOP_K]
    base = jnp.zeros_like(eid)
    for e in range(N_EXPERTS):
        base = jnp.where(eid == e, offs[e], base)
    pos = (base + rank).reshape(-1).astype(jnp.int32)
    tile_ends = ends // EM
    tile_ids = jnp.arange(n_tiles, dtype=jnp.int32)
    tile_expert = jnp.minimum(
        jnp.sum((tile_ends[None, :] <= tile_ids[:, None]).astype(jnp.int32), axis=1),
        N_EXPERTS - 1).astype(jnp.int32)
    n_used = tile_ends[-1:].astype(jnp.int32)
    trailing = tile_ends[-1] + jnp.arange(N_EXPERTS, dtype=jnp.int32)
    ztile = jnp.concatenate([
        jnp.where(padded > 0, tile_ends - 1, -1),
        jnp.where(trailing < n_tiles, trailing, -1)]).astype(jnp.int32)
    xs = _dispatch(pos, ztile, hf_tiles, n_tiles * EM)
    ys = _experts(layer, tile_expert, n_used, xs, w_gu, b_gu, w_down, b_down)
    return _combine(pos, ys, x_new, gates, out_vecs, final)


def _pad_rows(rows):
    rows = [r.reshape(1, -1) for r in rows]
    rows += [jnp.zeros_like(rows[0])] * (SUBLANES - len(rows))
    return jnp.concatenate(rows, axis=0)


def _rope_tables(S, C):
    quarter = HEAD_DIM // 4
    inv_freq = ROPE_THETA ** (-np.arange(quarter, dtype=np.float32) / quarter)
    n_rows = S // GRID_W
    ang_r = np.arange(n_rows, dtype=np.float32)[:, None] * inv_freq[None, :]
    ang_c = np.arange(GRID_W, dtype=np.float32)[:, None] * inv_freq[None, :]

    def expand(tab_r, tab_c, sign):
        r = jnp.repeat(jnp.asarray(tab_r, jnp.float32), GRID_W, axis=0)
        cc = jnp.tile(jnp.asarray(tab_c, jnp.float32), (n_rows, 1))
        return jnp.concatenate([sign * r, r, sign * cc, cc], axis=1)

    cos = expand(np.cos(ang_r), np.cos(ang_c), 1.0)
    sin = expand(np.sin(ang_r), np.sin(ang_c), -1.0)
    cos = jnp.concatenate([cos, jnp.ones((C, HEAD_DIM), jnp.float32)], axis=0)
    sin = jnp.concatenate([sin, jnp.zeros((C, HEAD_DIM), jnp.float32)], axis=0)
    return cos, sin


def kernel(x, c, ctx, c_ctx, ada_w, ada_b, norm_mix, norm_ffn, attn_w_qkv, attn_q_norm, attn_k_norm,
           attn_w_o, pool_w, pool_scale, moe_router_w, moe_router_b, moe_w_gu, moe_b_gu, moe_w_down,
           moe_b_down, final_norm):
    B, S, D = x.shape
    C = ctx.shape[1]
    assert B == 1 and D == D_MODEL and S % GRID_W == 0
    assert S % TM == 0 and C % TM == 0 and S % RM == 0 and S % AQ == 0 and (S * TOP_K) % EM == 0
    x2d = x.reshape(S, D)
    ctx2d = ctx.reshape(C, D)
    bf = jnp.bfloat16

    cvec = _pad_rows([c.reshape(-1), c_ctx])
    mod = _modulation(cvec, ada_w, ada_b)
    m_l = [mod[l, 0].reshape(N_MOD, D) for l in range(2)]
    m_c0 = mod[0, 1].reshape(N_MOD, D)

    def split_router(w):
        w = jnp.pad(w, ((0, 0), (0, LANES - N_EXPERTS)))
        w1 = w.astype(bf)
        r1 = w - w1.astype(jnp.float32)
        w2 = r1.astype(bf)
        w3 = (r1 - w2.astype(jnp.float32)).astype(bf)
        return jnp.concatenate([w1, w2, w3], axis=1)

    rw = [split_router(moe_router_w[l]) for l in range(2)]
    rb = [jnp.pad(moe_router_b[l], (0, LANES - N_EXPERTS), constant_values=NEG_BIG).reshape(1, LANES)
          for l in range(2)]

    cos_t, sin_t = _rope_tables(S, C)
    qkv_vecs = _pad_rows([m_l[0][0], m_l[0][1], m_c0[0], m_c0[1], norm_mix[0]])
    q, k, v = _qkv(x2d, ctx2d, qkv_vecs, attn_w_qkv[0].astype(bf), attn_q_norm[0].reshape(1, -1),
                   attn_k_norm[0].reshape(1, -1), cos_t, sin_t)
    o = _attention(q, k, v, S)
    vecs0 = _pad_rows([m_l[0][2], m_l[0][3], m_l[0][4], norm_ffn[0]])
    x1, hf, meta, gates, counts = _post_attn(x2d, o, attn_w_o[0].astype(bf), vecs0, rw[0], rb[0])
    x2 = _moe(0, x1, hf, meta, gates, counts, moe_w_gu, moe_b_gu, moe_w_down, moe_b_down,
              _pad_rows([m_l[0][5]]), final=False)

    mvecs = _pad_rows([m_l[1][0], m_l[1][1], norm_mix[1], pool_scale[0]])
    vecs1 = _pad_rows([m_l[1][2], m_l[1][3], m_l[1][4], norm_ffn[1]])
    x3, hf, meta, gates, counts = _pool(x2, mvecs, pool_w[0].astype(bf), vecs1, rw[1], rb[1])
    out = _moe(1, x3, hf, meta, gates, counts, moe_w_gu, moe_b_gu, moe_w_down, moe_b_down,
               _pad_rows([m_l[1][5], final_norm]), final=True)
    return out.reshape(B, S, D)
```

```python
import functools

import numpy as np
import jax
import jax.numpy as jnp
from jax import lax
from jax.experimental import pallas as pl
from jax.experimental.pallas import tpu as pltpu

D_MODEL = 1024
GRID_W = 64
N_HEADS = 8
N_KV_HEADS = 2
KV_GROUP = N_HEADS // N_KV_HEADS
HEAD_DIM = 128
ROPE_THETA = 10000.0
POOL_WINDOWS = (2, 4, 8, 16)
POOL_GROUP = D_MODEL // len(POOL_WINDOWS)
POOL_HALO = 8
N_EXPERTS = 32
TOP_K = 4
D_FF = D_MODEL
SWIGLU_LIMIT = 7.0
SWIGLU_ALPHA = 1.702
NORM_EPS = 1e-6
N_MOD = 6

LANES = 128
SUBLANES = 8
D_CHUNKS = D_MODEL // LANES
TM = 256
EM = 512
RM = 512
AQ = 512
TK_MAX = 1280
ATTN_ROW_SUM_FLOOR = 2.0 ** -100
ATTN_UNROLL = 2
VMEM_LIMIT = 48 * 1024 * 1024
ATTN_VMEM_LIMIT = 56 * 1024 * 1024
LOG2E = 1.4426950408889634
NEG_BIG = -1e30

_HI = lax.Precision.HIGHEST


def _cparams(sem):
    return pltpu.CompilerParams(dimension_semantics=sem, vmem_limit_bytes=VMEM_LIMIT)


def _rms(x, eps=NORM_EPS):
    return x * lax.rsqrt(jnp.mean(x * x, axis=-1, keepdims=True) + eps)


def _mod_kernel(cv_ref, w_ref, b_ref, o_ref):
    cv = cv_ref[...]
    s = cv / (1.0 + jnp.exp(-cv))
    o_ref[0] = jnp.dot(s, w_ref[0], precision=_HI, preferred_element_type=jnp.float32) + b_ref[0]


def _modulation(cvec, ada_w, ada_b):
    depth = ada_w.shape[0]
    nblk = 4
    bw = N_MOD * D_MODEL // nblk
    return pl.pallas_call(
        _mod_kernel,
        grid=(depth, nblk),
        in_specs=[
            pl.BlockSpec((SUBLANES, D_MODEL), lambda l, j: (0, 0)),
            pl.BlockSpec((1, D_MODEL, bw), lambda l, j: (l, 0, j)),
            pl.BlockSpec((1, 1, bw), lambda l, j: (l, 0, j)),
        ],
        out_specs=pl.BlockSpec((1, SUBLANES, bw), lambda l, j: (l, 0, j)),
        out_shape=jax.ShapeDtypeStruct((depth, SUBLANES, N_MOD * D_MODEL), jnp.float32),
        compiler_params=_cparams(("arbitrary", "arbitrary")),
        name="modulation",
    )(cvec, ada_w, ada_b.reshape(depth, 1, N_MOD * D_MODEL))


def _rope(x, cos, sin_signed):
    lane = lax.broadcasted_iota(jnp.int32, x.shape, 1)
    partner = jnp.where((lane % 64) < 32, pltpu.roll(x, LANES - 32, 1), pltpu.roll(x, 32, 1))
    return x * cos + partner * sin_signed


def _qkv_kernel(n_lat_tiles, x_ref, ctx_ref, vec_ref, w_ref, qg_ref, kg_ref, cos_ref, sin_ref,
                q_ref, k_ref, v_ref):
    i = pl.program_id(0)
    is_ctx = i >= n_lat_tiles
    xt = jnp.where(is_ctx, ctx_ref[...], x_ref[...])
    sh = jnp.where(is_ctx, vec_ref[2:3, :], vec_ref[0:1, :])
    sc = jnp.where(is_ctx, vec_ref[3:4, :], vec_ref[1:2, :])
    h = _rms(xt) * vec_ref[4:5, :] * (1.0 + sc) + sh
    qkv = jnp.dot(h.astype(jnp.bfloat16), w_ref[...], preferred_element_type=jnp.float32)
    cos = cos_ref[...]
    sin = sin_ref[...]
    q_scale = (HEAD_DIM ** -0.5) * LOG2E
    for hd in range(N_HEADS):
        qh = _rms(qkv[:, hd * HEAD_DIM:(hd + 1) * HEAD_DIM]) * qg_ref[...]
        q_ref[hd] = (_rope(qh, cos, sin) * q_scale).astype(jnp.bfloat16)
    for kv in range(N_KV_HEADS):
        c0 = (N_HEADS + kv) * HEAD_DIM
        kh = _rms(qkv[:, c0:c0 + HEAD_DIM]) * kg_ref[...]
        k_ref[kv] = _rope(kh, cos, sin).astype(jnp.bfloat16)
        c1 = (N_HEADS + N_KV_HEADS + kv) * HEAD_DIM
        v_ref[kv] = qkv[:, c1:c1 + HEAD_DIM].astype(jnp.bfloat16)


def _qkv(x, ctx, vecs, w_qkv, q_g, k_g, cos_t, sin_t):
    S, C = x.shape[0], ctx.shape[0]
    n_lat = S // TM
    n_ctx = C // TM
    T = S + C
    last = n_lat - 1
    qkv_w = w_qkv.shape[1]
    out_shapes = (
        jax.ShapeDtypeStruct((N_HEADS, T, HEAD_DIM), jnp.bfloat16),
        jax.ShapeDtypeStruct((N_KV_HEADS, T, HEAD_DIM), jnp.bfloat16),
        jax.ShapeDtypeStruct((N_KV_HEADS, T, HEAD_DIM), jnp.bfloat16),
    )
    return pl.pallas_call(
        functools.partial(_qkv_kernel, n_lat),
        grid=(n_lat + n_ctx,),
        in_specs=[
            pl.BlockSpec((TM, D_MODEL), lambda i: (jnp.minimum(i, last), 0)),
            pl.BlockSpec((TM, D_MODEL), lambda i: (jnp.maximum(i - n_lat, 0), 0)),
            pl.BlockSpec((SUBLANES, D_MODEL), lambda i: (0, 0)),
            pl.BlockSpec((D_MODEL, qkv_w), lambda i: (0, 0)),
            pl.BlockSpec((1, HEAD_DIM), lambda i: (0, 0)),
            pl.BlockSpec((1, HEAD_DIM), lambda i: (0, 0)),
            pl.BlockSpec((TM, HEAD_DIM), lambda i: (i, 0)),
            pl.BlockSpec((TM, HEAD_DIM), lambda i: (i, 0)),
        ],
        out_specs=(
            pl.BlockSpec((N_HEADS, TM, HEAD_DIM), lambda i: (0, i, 0)),
            pl.BlockSpec((N_KV_HEADS, TM, HEAD_DIM), lambda i: (0, i, 0)),
            pl.BlockSpec((N_KV_HEADS, TM, HEAD_DIM), lambda i: (0, i, 0)),
        ),
        out_shape=out_shapes,
        compiler_params=_cparams(("arbitrary",)),
        name="qkv_proj",
    )(x, ctx, vecs, w_qkv, q_g, k_g, cos_t, sin_t)


def _lane_repeat(x, n):
    return jnp.concatenate([x] * n, axis=1)


def _attn_kernel(n_chunks, tk, q_ref, k_ref, v_ref, o_ref, s_ref, p_ref, a_ref, mc_ref, m_ref, acc_ref):
    rows = KV_GROUP * TM
    ones = jnp.ones((tk, HEAD_DIM), jnp.bfloat16)

    def scores(c, slot):
        k = k_ref[0, pl.ds(pl.multiple_of(c * tk, tk), tk), :]
        q = q_ref[...].reshape(rows, HEAD_DIM)
        s = lax.dot_general(q, k, (((1,), (1,)), ((), ())), preferred_element_type=jnp.float32)
        s_ref[slot] = s
        mc_ref[slot] = jnp.broadcast_to(jnp.max(s, axis=1, keepdims=True), (rows, LANES))

    def softmax(slot):
        m_old = m_ref[...]
        m_new = jnp.maximum(m_old, mc_ref[slot])
        a_ref[slot] = jnp.exp2(m_old - m_new)
        m_ref[...] = m_new
        p_ref[slot] = jnp.exp2(s_ref[slot] - _lane_repeat(m_new, tk // LANES)).astype(jnp.bfloat16)

    def values(c, slot):
        v = v_ref[0, pl.ds(pl.multiple_of(c * tk, tk), tk), :]
        vx = jnp.concatenate([v, ones], axis=1)
        acc_ref[...] = (acc_ref[...] * _lane_repeat(a_ref[slot], 2)
                        + jnp.dot(p_ref[slot], vx, preferred_element_type=jnp.float32))

    def iteration(t, par):
        scores(t + 1, 1 - par)
        softmax(par)
        values(jnp.maximum(t - 1, 0), 1 - par)

    m_ref[...] = jnp.full(m_ref.shape, -jnp.inf, jnp.float32)
    acc_ref[...] = jnp.zeros(acc_ref.shape, jnp.float32)
    p_ref[1] = jnp.zeros(p_ref.shape[1:], jnp.bfloat16)
    a_ref[1] = jnp.ones(a_ref.shape[1:], jnp.float32)
    scores(0, 0)

    n_steady = n_chunks - 1

    def group(u, carry):
        for j in range(ATTN_UNROLL):
            iteration(ATTN_UNROLL * u + j, j % 2)
        return carry

    n_groups = n_steady // ATTN_UNROLL
    lax.fori_loop(0, n_groups, group, 0)
    for t in range(n_groups * ATTN_UNROLL, n_steady):
        iteration(t, t % 2)
    last = n_chunks - 1
    softmax(last % 2)
    values(max(last - 1, 0), 1 - last % 2)
    values(last, last % 2)

    acc = acc_ref[...]
    o = acc[:, :HEAD_DIM] / acc[:, HEAD_DIM:]
    for g in range(KV_GROUP):
        o_ref[:, g * HEAD_DIM:(g + 1) * HEAD_DIM] = o[g * TM:(g + 1) * TM].astype(o_ref.dtype)


def _key_chunk(T):
    return max(c for c in range(LANES, TK_MAX + 1, LANES) if T % c == 0)


def _attn_bound_kernel(n_chunks, tk, q_ref, k_ref, v_ref, o_ref, lmin_ref, p_ref, m_ref, acc_ref, kmax_ref):
    i = pl.program_id(1)
    rows = KV_GROUP * AQ
    ones = jnp.ones((tk, HEAD_DIM), jnp.bfloat16)

    @pl.when(i == 0)
    def _():
        def body(c, mx):
            k = k_ref[0, pl.ds(pl.multiple_of(c * tk, tk), tk), :].astype(jnp.float32)
            n2 = jnp.sum(k * k, axis=1, keepdims=True)
            return jnp.maximum(mx, jnp.max(n2, axis=0, keepdims=True))

        kmax2 = lax.fori_loop(0, n_chunks, body, jnp.zeros((1, 1), jnp.float32))
        kmax_ref[...] = jnp.broadcast_to(kmax2, kmax_ref.shape)

    qf = q_ref[...].reshape(rows, HEAD_DIM).astype(jnp.float32)
    qn2 = jnp.sum(qf * qf, axis=1, keepdims=True)
    m_ref[...] = jnp.broadcast_to(jnp.sqrt(qn2 * kmax_ref[0:1, 0:1]), (rows, LANES))
    acc_ref[...] = jnp.zeros(acc_ref.shape, jnp.float32)

    def probs(c, slot):
        k = k_ref[0, pl.ds(pl.multiple_of(c * tk, tk), tk), :]
        q = q_ref[...].reshape(rows, HEAD_DIM)
        s = lax.dot_general(q, k, (((1,), (1,)), ((), ())), preferred_element_type=jnp.float32)
        p_ref[slot] = jnp.exp2(s - _lane_repeat(m_ref[...], tk // LANES)).astype(jnp.bfloat16)

    def values(c, slot):
        v = v_ref[0, pl.ds(pl.multiple_of(c * tk, tk), tk), :]
        vx = jnp.concatenate([v, ones], axis=1)
        acc_ref[...] += jnp.dot(p_ref[slot], vx, preferred_element_type=jnp.float32)

    def iteration(t, par):
        probs(t + 1, 1 - par)
        values(t, par)

    probs(0, 0)
    n_steady = n_chunks - 1

    def group(u, carry):
        for j in range(ATTN_UNROLL):
            iteration(ATTN_UNROLL * u + j, j % 2)
        return carry

    n_groups = n_steady // ATTN_UNROLL
    lax.fori_loop(0, n_groups, group, 0)
    for t in range(n_groups * ATTN_UNROLL, n_steady):
        iteration(t, t % 2)
    values(n_chunks - 1, (n_chunks - 1) % 2)

    acc = acc_ref[...]
    l = acc[:, HEAD_DIM:]
    o = acc[:, :HEAD_DIM] / l
    for g in range(KV_GROUP):
        o_ref[:, g * HEAD_DIM:(g + 1) * HEAD_DIM] = o[g * AQ:(g + 1) * AQ].astype(o_ref.dtype)
    lmin_ref[0] = jnp.broadcast_to(jnp.min(l, axis=0, keepdims=True), (SUBLANES, LANES))


def _attention_bound(q, k, v, S):
    T = k.shape[1]
    tk = _key_chunk(T)
    rows = KV_GROUP * AQ
    n_q = S // AQ
    return pl.pallas_call(
        functools.partial(_attn_bound_kernel, T // tk, tk),
        grid=(N_KV_HEADS, n_q),
        in_specs=[
            pl.BlockSpec((KV_GROUP, AQ, HEAD_DIM), lambda g, i: (g, i, 0)),
            pl.BlockSpec((1, T, HEAD_DIM), lambda g, i: (g, 0, 0)),
            pl.BlockSpec((1, T, HEAD_DIM), lambda g, i: (g, 0, 0)),
        ],
        out_specs=(
            pl.BlockSpec((AQ, KV_GROUP * HEAD_DIM), lambda g, i: (i, g)),
            pl.BlockSpec((1, SUBLANES, LANES), lambda g, i: (g * n_q + i, 0, 0)),
        ),
        out_shape=(
            jax.ShapeDtypeStruct((S, N_HEADS * HEAD_DIM), jnp.bfloat16),
            jax.ShapeDtypeStruct((N_KV_HEADS * n_q, SUBLANES, LANES), jnp.float32),
        ),
        scratch_shapes=[
            pltpu.VMEM((2, rows, tk), jnp.bfloat16),
            pltpu.VMEM((rows, LANES), jnp.float32),
            pltpu.VMEM((rows, 2 * HEAD_DIM), jnp.float32),
            pltpu.VMEM((SUBLANES, LANES), jnp.float32),
        ],
        compiler_params=pltpu.CompilerParams(dimension_semantics=("arbitrary", "arbitrary"),
                                             vmem_limit_bytes=ATTN_VMEM_LIMIT),
        name="flash_attention_bound",
    )(q, k, v)


def _attention(q, k, v, S):
    o, lmin = _attention_bound(q, k, v, S)
    ok = jnp.min(lmin) >= ATTN_ROW_SUM_FLOOR
    return lax.cond(ok, lambda: o, lambda: _attention_running_max(q, k, v, S))


def _attention_running_max(q, k, v, S):
    T = k.shape[1]
    tk = _key_chunk(T)
    rows = KV_GROUP * TM
    return pl.pallas_call(
        functools.partial(_attn_kernel, T // tk, tk),
        grid=(N_KV_HEADS, S // TM),
        in_specs=[
            pl.BlockSpec((KV_GROUP, TM, HEAD_DIM), lambda g, i: (g, i, 0)),
            pl.BlockSpec((1, T, HEAD_DIM), lambda g, i: (g, 0, 0)),
            pl.BlockSpec((1, T, HEAD_DIM), lambda g, i: (g, 0, 0)),
        ],
        out_specs=pl.BlockSpec((TM, KV_GROUP * HEAD_DIM), lambda g, i: (i, g)),
        out_shape=jax.ShapeDtypeStruct((S, N_HEADS * HEAD_DIM), jnp.bfloat16),
        scratch_shapes=[
            pltpu.VMEM((2, rows, tk), jnp.float32),
            pltpu.VMEM((2, rows, tk), jnp.bfloat16),
            pltpu.VMEM((2, rows, LANES), jnp.float32),
            pltpu.VMEM((2, rows, LANES), jnp.float32),
            pltpu.VMEM((rows, LANES), jnp.float32),
            pltpu.VMEM((rows, 2 * HEAD_DIM), jnp.float32),
        ],
        compiler_params=pltpu.CompilerParams(dimension_semantics=("arbitrary", "arbitrary"),
                                             vmem_limit_bytes=ATTN_VMEM_LIMIT),
        name="flash_attention",
    )(q, k, v)


def _router_out_shapes(S):
    return (
        jax.ShapeDtypeStruct((S, D_MODEL), jnp.float32),
        jax.ShapeDtypeStruct((S * SUBLANES, LANES), jnp.float32),
        jax.ShapeDtypeStruct((SUBLANES, S), jnp.int32),
        jax.ShapeDtypeStruct((S, LANES), jnp.float32),
        jax.ShapeDtypeStruct((SUBLANES, LANES), jnp.int32),
    )


def _router_out_specs():
    return (
        pl.BlockSpec((RM, D_MODEL), lambda i: (i, 0)),
        pl.BlockSpec((RM * SUBLANES, LANES), lambda i: (i, 0)),
        pl.BlockSpec((SUBLANES, RM), lambda i: (0, i)),
        pl.BlockSpec((RM, LANES), lambda i: (i, 0)),
        pl.BlockSpec((SUBLANES, LANES), lambda i: (0, 0)),
    )


def _router_core(x_new, vec_ref, rw_ref, rb_ref, cnt_ref, x_out_ref, hf_ref, meta_ref, gate_ref,
                 counts_ref):
    i = pl.program_id(0)

    @pl.when(i == 0)
    def _():
        cnt_ref[...] = jnp.zeros_like(cnt_ref)

    x_out_ref[...] = x_new
    hf = _rms(x_new) * vec_ref[3:4, :] * (1.0 + vec_ref[2:3, :]) + vec_ref[1:2, :]
    for c in range(D_CHUNKS):
        hf_ref[pl.ds(c, RM, stride=SUBLANES), :] = hf[:, c * LANES:(c + 1) * LANES]

    h1 = hf.astype(jnp.bfloat16)
    r1 = hf - h1.astype(jnp.float32)
    h2 = r1.astype(jnp.bfloat16)
    h3 = (r1 - h2.astype(jnp.float32)).astype(jnp.bfloat16)
    w12 = rw_ref[:, :2 * LANES]
    cross = (jnp.dot(h1, w12, preferred_element_type=jnp.float32)
             + jnp.dot(h2, w12, preferred_element_type=jnp.float32))
    small = (jnp.dot(h3, rw_ref[:, :LANES], preferred_element_type=jnp.float32)
             + jnp.dot(h1, rw_ref[:, 2 * LANES:], preferred_element_type=jnp.float32))
    logits = cross[:, :LANES] + cross[:, LANES:] + small + rb_ref[...]
    lane = lax.broadcasted_iota(jnp.int32, logits.shape, 1)
    lane_f = lane.astype(jnp.float32)
    vals, idxs, hots = [], [], []
    work = logits
    for _ in range(TOP_K):
        mx = jnp.max(work, axis=1, keepdims=True)
        idx = jnp.min(jnp.where(work == mx, lane_f, float(LANES)), axis=1, keepdims=True)
        hot = lane_f == idx
        work = jnp.where(hot, -jnp.inf, work)
        vals.append(mx)
        idxs.append(idx.astype(jnp.int32))
        hots.append(hot)
    exps = [jnp.exp(vk - vals[0]) for vk in vals]
    denom = exps[0] + exps[1] + exps[2] + exps[3]

    any_hot = hots[0] | hots[1] | hots[2] | hots[3]
    hot_f = jnp.where(any_hot, 1.0, 0.0)
    r_i = lax.broadcasted_iota(jnp.int32, (RM, RM), 0)
    c_i = lax.broadcasted_iota(jnp.int32, (RM, RM), 1)
    lower = jnp.where(c_i < r_i, 1.0, 0.0).astype(jnp.bfloat16)
    before = jnp.dot(lower, hot_f.astype(jnp.bfloat16), preferred_element_type=jnp.float32) + cnt_ref[0:1, :]
    cnt_ref[0:1, :] = cnt_ref[0:1, :] + jnp.sum(hot_f, axis=0, keepdims=True)

    meta = jnp.zeros(logits.shape, jnp.float32)
    gates = jnp.zeros(logits.shape, jnp.float32)
    for k in range(TOP_K):
        rank = jnp.sum(jnp.where(hots[k], before, 0.0), axis=1, keepdims=True)
        meta = jnp.where(lane == k, idxs[k].astype(jnp.float32), meta)
        meta = jnp.where(lane == TOP_K + k, rank, meta)
        gates = jnp.where(lane == k, exps[k] / denom, gates)
    meta_ref[...] = meta.T[:SUBLANES, :].astype(jnp.int32)
    gate_ref[...] = gates
    counts_ref[...] = jnp.broadcast_to(cnt_ref[0:1, :], counts_ref.shape).astype(jnp.int32)


def _post_attn_kernel(x_ref, o_ref, wo_ref, vec_ref, rw_ref, rb_ref,
                      x_out_ref, hf_ref, meta_ref, gate_ref, counts_ref, cnt_ref):
    y = jnp.dot(o_ref[...], wo_ref[...], preferred_element_type=jnp.float32)
    x_new = x_ref[...] + vec_ref[0:1, :] * y
    _router_core(x_new, vec_ref, rw_ref, rb_ref, cnt_ref, x_out_ref, hf_ref, meta_ref, gate_ref,
                 counts_ref)


def _post_attn(x, o, w_o, vecs, rw, rb):
    S = x.shape[0]
    return pl.pallas_call(
        _post_attn_kernel,
        grid=(S // RM,),
        in_specs=[
            pl.BlockSpec((RM, D_MODEL), lambda i: (i, 0)),
            pl.BlockSpec((RM, D_MODEL), lambda i: (i, 0)),
            pl.BlockSpec((D_MODEL, D_MODEL), lambda i: (0, 0)),
            pl.BlockSpec((SUBLANES, D_MODEL), lambda i: (0, 0)),
            pl.BlockSpec((D_MODEL, 3 * LANES), lambda i: (0, 0)),
            pl.BlockSpec((1, LANES), lambda i: (0, 0)),
        ],
        out_specs=_router_out_specs(),
        out_shape=_router_out_shapes(S),
        scratch_shapes=[pltpu.VMEM((SUBLANES, LANES), jnp.float32)],
        compiler_params=_cparams(("arbitrary",)),
        name="post_attn_router",
    )(x, o, w_o, vecs, rw, rb)


def _pool_kernel(S, x_ref, prev_ref, next_ref, mvec_ref, pw_ref, vec_ref, rw_ref, rb_ref,
                 x_out_ref, hf_ref, meta_ref, gate_ref, counts_ref, cnt_ref, buf_ref):
    i = pl.program_id(0)
    n = pl.num_programs(0)

    def hmod(xx):
        return _rms(xx) * mvec_ref[2:3, :] * (1.0 + mvec_ref[1:2, :]) + mvec_ref[0:1, :]

    x_cur = x_ref[...]
    h_cur = hmod(x_cur)
    buf_ref[0:POOL_HALO, :] = jnp.where(i > 0, hmod(prev_ref[...]), 0.0)
    buf_ref[POOL_HALO:POOL_HALO + RM, :] = h_cur
    buf_ref[POOL_HALO + RM:, :] = jnp.where(i < n - 1, hmod(next_ref[...]), 0.0)

    t = i * RM + lax.broadcasted_iota(jnp.int32, (RM, 1), 0)
    ys = []
    for g, w in enumerate(POOL_WINDOWS):
        c0 = g * POOL_GROUP
        acc = None
        for d in range(-(w // 2), w - w // 2):
            piece = buf_ref[POOL_HALO + d:POOL_HALO + d + RM, c0:c0 + POOL_GROUP]
            acc = piece if acc is None else acc + piece
        cnt = jnp.minimum(t - w // 2 + w, S) - jnp.maximum(t - w // 2, 0)
        pooled = acc / cnt.astype(jnp.float32) - h_cur[:, c0:c0 + POOL_GROUP]
        ys.append(jnp.dot(pooled.astype(jnp.bfloat16), pw_ref[g], preferred_element_type=jnp.float32))
    y = jnp.concatenate(ys, axis=1) * mvec_ref[3:4, :]
    x_new = x_cur + vec_ref[0:1, :] * y
    _router_core(x_new, vec_ref, rw_ref, rb_ref, cnt_ref, x_out_ref, hf_ref, meta_ref, gate_ref,
                 counts_ref)


def _pool(x, mvecs, pool_w, vecs, rw, rb):
    S = x.shape[0]
    hb = RM // POOL_HALO
    n_halo = S // POOL_HALO
    return pl.pallas_call(
        functools.partial(_pool_kernel, S),
        grid=(S // RM,),
        in_specs=[
            pl.BlockSpec((RM, D_MODEL), lambda i: (i, 0)),
            pl.BlockSpec((POOL_HALO, D_MODEL), lambda i: (jnp.maximum(i * hb - 1, 0), 0)),
            pl.BlockSpec((POOL_HALO, D_MODEL), lambda i: (jnp.minimum((i + 1) * hb, n_halo - 1), 0)),
            pl.BlockSpec((SUBLANES, D_MODEL), lambda i: (0, 0)),
            pl.BlockSpec((len(POOL_WINDOWS), POOL_GROUP, POOL_GROUP), lambda i: (0, 0, 0)),
            pl.BlockSpec((SUBLANES, D_MODEL), lambda i: (0, 0)),
            pl.BlockSpec((D_MODEL, 3 * LANES), lambda i: (0, 0)),
            pl.BlockSpec((1, LANES), lambda i: (0, 0)),
        ],
        out_specs=_router_out_specs(),
        out_shape=_router_out_shapes(S),
        scratch_shapes=[
            pltpu.VMEM((SUBLANES, LANES), jnp.float32),
            pltpu.VMEM((RM + 2 * POOL_HALO, D_MODEL), jnp.float32),
        ],
        compiler_params=_cparams(("arbitrary",)),
        name="pool_router",
    )(x, x, x, mvecs, pool_w, vecs, rw, rb)


def _dispatch_kernel(pos_ref, ztile_ref, hf_ref, xs_ref, stage_ref, zeros_ref, load_sem, row_sem, zero_sem):
    i = pl.program_id(0)
    n = pl.num_programs(0)
    tile_rows = TM * SUBLANES
    pairs = TM * TOP_K

    def load(tile, slot):
        src = pl.multiple_of(tile * tile_rows, tile_rows)
        dst = pl.multiple_of(slot * tile_rows, tile_rows)
        return pltpu.make_async_copy(hf_ref.at[pl.ds(src, tile_rows)],
                                     stage_ref.at[pl.ds(dst, tile_rows)], load_sem.at[slot])

    def rows_wait():
        pltpu.make_async_copy(xs_ref.at[pl.ds(0, pairs * SUBLANES)],
                              xs_ref.at[pl.ds(0, pairs * SUBLANES)], row_sem).wait()

    def zero_copy(e):
        zero_rows = EM * SUBLANES
        start = pl.multiple_of(jnp.maximum(ztile_ref[e], 0) * zero_rows, zero_rows)
        return pltpu.make_async_copy(zeros_ref, xs_ref.at[pl.ds(start, zero_rows)], zero_sem)

    @pl.when(i == 0)
    def _():
        zeros_ref[...] = jnp.zeros_like(zeros_ref)
        load(0, 0).start()

        def zero_start(e, carry):
            @pl.when(ztile_ref[e] >= 0)
            def _():
                zero_copy(e).start()
            return carry

        def zero_wait(e, carry):
            @pl.when(ztile_ref[e] >= 0)
            def _():
                zero_copy(e).wait()
            return carry

        lax.fori_loop(0, 2 * N_EXPERTS, zero_start, 0)
        lax.fori_loop(0, 2 * N_EXPERTS, zero_wait, 0)

    @pl.when(i > 0)
    def _():
        rows_wait()

    @pl.when(i + 1 < n)
    def _():
        load(i + 1, (i + 1) % 2).start()

    slot = i % 2
    load(i, slot).wait()

    def issue(r, carry):
        src = pl.multiple_of(slot * tile_rows + r * SUBLANES, SUBLANES)
        for k in range(TOP_K):
            dst = pl.multiple_of(pos_ref[k * (n * TM) + i * TM + r] * SUBLANES, SUBLANES)
            pltpu.make_async_copy(stage_ref.at[pl.ds(src, SUBLANES)],
                                  xs_ref.at[pl.ds(dst, SUBLANES)], row_sem).start(priority=k % 2)
        return carry

    lax.fori_loop(0, TM, issue, 0)

    @pl.when(i == n - 1)
    def _():
        rows_wait()


def _dispatch(pos, ztile, hf_tiles, n_rows):
    S = hf_tiles.shape[0] // SUBLANES
    tile_rows = TM * SUBLANES
    return pl.pallas_call(
        _dispatch_kernel,
        grid_spec=pltpu.PrefetchScalarGridSpec(
            num_scalar_prefetch=2,
            grid=(S // TM,),
            in_specs=[pl.BlockSpec(memory_space=pl.ANY)],
            out_specs=pl.BlockSpec(memory_space=pl.ANY),
            scratch_shapes=[
                pltpu.VMEM((2 * tile_rows, LANES), jnp.float32),
                pltpu.VMEM((EM * SUBLANES, LANES), jnp.float32),
                pltpu.SemaphoreType.DMA((2,)),
                pltpu.SemaphoreType.DMA(()),
                pltpu.SemaphoreType.DMA(()),
            ],
        ),
        out_shape=jax.ShapeDtypeStruct((n_rows * SUBLANES, LANES), jnp.float32),
        compiler_params=_cparams(("arbitrary",)),
        name="moe_dispatch",
    )(pos, ztile, hf_tiles)


W_CAST_ROWS = 64


def _expert_kernel(layer, te_ref, nused_ref, gidx_ref, nexte_ref, xs_ref, wgu_hbm, bgu_ref, wd_hbm,
                   bd_ref, ys_ref, wgu_f32_ref, wd_f32_ref, wgu_bf_ref, wd_bf_ref, sem):
    i = pl.program_id(0)
    used = i < nused_ref[0]
    new_expert = (i == 0) | (te_ref[i] != te_ref[jnp.maximum(i - 1, 0)])
    slot = gidx_ref[i] % 2

    def fetch(e, s):
        return (pltpu.make_async_copy(wgu_hbm.at[layer, e], wgu_f32_ref.at[s], sem.at[s, 0]),
                pltpu.make_async_copy(wd_hbm.at[layer, e], wd_f32_ref.at[s], sem.at[s, 1]))

    @pl.when(i == 0)
    def _():
        for c in fetch(te_ref[0], 0):
            c.start()

    @pl.when(used & new_expert)
    def _():
        for c in fetch(te_ref[i], slot):
            c.wait()

        @pl.when(nexte_ref[i] >= 0)
        def _():
            for c in fetch(nexte_ref[i], 1 - slot):
                c.start()

        def cast_gu(r, carry):
            rows = pl.ds(pl.multiple_of(r * W_CAST_ROWS, W_CAST_ROWS), W_CAST_ROWS)
            wgu_bf_ref[rows, :] = wgu_f32_ref[slot, rows, :].astype(jnp.bfloat16)
            return carry

        def cast_d(r, carry):
            rows = pl.ds(pl.multiple_of(r * W_CAST_ROWS, W_CAST_ROWS), W_CAST_ROWS)
            wd_bf_ref[rows, :] = wd_f32_ref[slot, rows, :].astype(jnp.bfloat16)
            return carry

        lax.fori_loop(0, D_MODEL // W_CAST_ROWS, cast_gu, 0)
        lax.fori_loop(0, D_FF // W_CAST_ROWS, cast_d, 0)

    @pl.when(used)
    def _():
        cols = [xs_ref[pl.ds(c, EM, stride=SUBLANES), :] for c in range(D_CHUNKS)]
        x = jnp.concatenate(cols, axis=1).astype(jnp.bfloat16)
        gu = jnp.dot(x, wgu_bf_ref[...], preferred_element_type=jnp.float32) + bgu_ref[...]
        gate = jnp.minimum(gu[:, :D_FF], SWIGLU_LIMIT)
        up = jnp.clip(gu[:, D_FF:], -SWIGLU_LIMIT, SWIGLU_LIMIT)
        glu = gate / (1.0 + jnp.exp(-SWIGLU_ALPHA * gate))
        a = ((up + 1.0) * glu).astype(jnp.bfloat16)
        y = jnp.dot(a, wd_bf_ref[...], preferred_element_type=jnp.float32) + bd_ref[...]
        for c in range(D_CHUNKS):
            ys_ref[pl.ds(c, EM, stride=SUBLANES), :] = y[:, c * LANES:(c + 1) * LANES]

    @pl.when(jnp.logical_not(used))
    def _():
        ys_ref[...] = jnp.zeros_like(ys_ref)


def _experts(layer, tile_expert, n_used, group_idx, next_expert, xs, w_gu, b_gu, w_down, b_down):
    n_tiles = tile_expert.shape[0]
    L, E = w_gu.shape[:2]

    def row_map(i, te, nu, gi, ne):
        return (jnp.minimum(i, nu[0] - 1), 0)

    def b_map(i, te, nu, gi, ne):
        return (layer, te[i], 0, 0)

    return pl.pallas_call(
        functools.partial(_expert_kernel, layer),
        grid_spec=pltpu.PrefetchScalarGridSpec(
            num_scalar_prefetch=4,
            grid=(n_tiles,),
            in_specs=[
                pl.BlockSpec((EM * SUBLANES, LANES), row_map),
                pl.BlockSpec(memory_space=pl.ANY),
                pl.BlockSpec((None, None, 1, 2 * D_FF), b_map),
                pl.BlockSpec(memory_space=pl.ANY),
                pl.BlockSpec((None, None, 1, D_MODEL), b_map),
            ],
            out_specs=pl.BlockSpec((EM * SUBLANES, LANES), lambda i, te, nu, gi, ne: (i, 0)),
            scratch_shapes=[
                pltpu.VMEM((2, D_MODEL, 2 * D_FF), jnp.float32),
                pltpu.VMEM((2, D_FF, D_MODEL), jnp.float32),
                pltpu.VMEM((D_MODEL, 2 * D_FF), jnp.bfloat16),
                pltpu.VMEM((D_FF, D_MODEL), jnp.bfloat16),
                pltpu.SemaphoreType.DMA((2, 2)),
            ],
        ),
        out_shape=jax.ShapeDtypeStruct(xs.shape, jnp.float32),
        compiler_params=_cparams(("arbitrary",)),
        name="moe_experts",
    )(tile_expert, n_used, group_idx, next_expert, xs, w_gu, b_gu.reshape(L, E, 1, 2 * D_FF), w_down,
      b_down.reshape(L, E, 1, D_MODEL))


def _combine_kernel(final, pos_ref, ys_ref, x_ref, gate_ref, vec_ref, o_ref, buf_ref, sem):
    i = pl.program_id(0)
    n = pl.num_programs(0)
    slot_rows = TOP_K * TM * SUBLANES

    def issue(tile, slot):
        def one(r, carry):
            for k in range(TOP_K):
                src = pl.multiple_of(pos_ref[k * (n * TM) + tile * TM + r] * SUBLANES, SUBLANES)
                dst = pl.multiple_of(slot * slot_rows + (k * TM + r) * SUBLANES, SUBLANES)
                pltpu.make_async_copy(ys_ref.at[pl.ds(src, SUBLANES)],
                                      buf_ref.at[pl.ds(dst, SUBLANES)], sem.at[slot]).start(priority=k % 2)
            return carry

        lax.fori_loop(0, TM, one, 0)

    @pl.when(i == 0)
    def _():
        issue(0, 0)

    @pl.when(i + 1 < n)
    def _():
        issue(i + 1, (i + 1) % 2)

    slot = i % 2
    base = pl.multiple_of(slot * slot_rows, slot_rows)
    pltpu.make_async_copy(ys_ref.at[pl.ds(0, slot_rows)], buf_ref.at[pl.ds(base, slot_rows)],
                          sem.at[slot]).wait()

    gates = gate_ref[...]
    pieces = []
    for c in range(D_CHUNKS):
        f = None
        for k in range(TOP_K):
            rows = buf_ref[pl.ds(base + k * TM * SUBLANES + c, TM, stride=SUBLANES), :]
            term = gates[:, k:k + 1] * rows
            f = term if f is None else f + term
        pieces.append(f)
    f = jnp.concatenate(pieces, axis=1)
    x_new = x_ref[...] + vec_ref[0:1, :] * f
    if final:
        x_new = _rms(x_new) * vec_ref[1:2, :]
    o_ref[...] = x_new


def _combine(pos, ys, x, gates, vecs, final):
    S = x.shape[0]
    return pl.pallas_call(
        functools.partial(_combine_kernel, final),
        grid_spec=pltpu.PrefetchScalarGridSpec(
            num_scalar_prefetch=1,
            grid=(S // TM,),
            in_specs=[
                pl.BlockSpec(memory_space=pl.ANY),
                pl.BlockSpec((TM, D_MODEL), lambda i, p: (i, 0)),
                pl.BlockSpec((TM, LANES), lambda i, p: (i, 0)),
                pl.BlockSpec((SUBLANES, D_MODEL), lambda i, p: (0, 0)),
            ],
            out_specs=pl.BlockSpec((TM, D_MODEL), lambda i, p: (i, 0)),
            scratch_shapes=[
                pltpu.VMEM((2 * TOP_K * TM * SUBLANES, LANES), jnp.float32),
                pltpu.SemaphoreType.DMA((2,)),
            ],
        ),
        out_shape=jax.ShapeDtypeStruct((S, D_MODEL), jnp.float32),
        compiler_params=_cparams(("arbitrary",)),
        name="moe_combine",
    )(pos, ys, x, gates, vecs)


def _moe(layer, x_new, hf_tiles, meta, gates, counts, w_gu, b_gu, w_down, b_down, out_vecs, final):
    S = x_new.shape[0]
    n_tiles = (S * TOP_K) // EM + N_EXPERTS
    cnt = counts[0, :N_EXPERTS]
    padded = ((cnt + EM - 1) // EM) * EM
    ends = jnp.cumsum(padded)
    offs = ends - padded
    eid = meta[:TOP_K]
    rank = meta[TOP_K:2 * TOP_K]
    base = jnp.zeros_like(eid)
    for e in range(N_EXPERTS):
        base = jnp.where(eid == e, offs[e], base)
    pos = (base + rank).reshape(-1).astype(jnp.int32)
    tile_ends = ends // EM
    tile_ids = jnp.arange(n_tiles, dtype=jnp.int32)
    tile_expert = jnp.minimum(
        jnp.sum((tile_ends[None, :] <= tile_ids[:, None]).astype(jnp.int32), axis=1),
        N_EXPERTS - 1).astype(jnp.int32)
    n_used = tile_ends[-1:].astype(jnp.int32)
    trailing = tile_ends[-1] + jnp.arange(N_EXPERTS, dtype=jnp.int32)
    ztile = jnp.concatenate([
        jnp.where(padded > 0, tile_ends - 1, -1),
        jnp.where(trailing < n_tiles, trailing, -1)]).astype(jnp.int32)
    xs = _dispatch(pos, ztile, hf_tiles, n_tiles * EM)
    experts = jnp.arange(N_EXPERTS, dtype=jnp.int32)
    nonempty = padded > 0
    before = nonempty[None, :] & (experts[None, :] < tile_expert[:, None])
    after = nonempty[None, :] & (experts[None, :] > tile_expert[:, None])
    group_idx = jnp.sum(before.astype(jnp.int32), axis=1)
    next_expert = jnp.min(jnp.where(after, experts[None, :], N_EXPERTS), axis=1)
    next_expert = jnp.where(next_expert < N_EXPERTS, next_expert, -1).astype(jnp.int32)
    ys = _experts(layer, tile_expert, n_used, group_idx, next_expert, xs, w_gu, b_gu, w_down, b_down)
    return _combine(pos, ys, x_new, gates, out_vecs, final)


def _pad_rows(rows):
    rows = [r.reshape(1, -1) for r in rows]
    rows += [jnp.zeros_like(rows[0])] * (SUBLANES - len(rows))
    return jnp.concatenate(rows, axis=0)


def _rope_tables(S, C):
    quarter = HEAD_DIM // 4
    inv_freq = ROPE_THETA ** (-np.arange(quarter, dtype=np.float32) / quarter)
    n_rows = S // GRID_W
    ang_r = np.arange(n_rows, dtype=np.float32)[:, None] * inv_freq[None, :]
    ang_c = np.arange(GRID_W, dtype=np.float32)[:, None] * inv_freq[None, :]

    def expand(tab_r, tab_c, sign):
        r = jnp.repeat(jnp.asarray(tab_r, jnp.float32), GRID_W, axis=0)
        cc = jnp.tile(jnp.asarray(tab_c, jnp.float32), (n_rows, 1))
        return jnp.concatenate([sign * r, r, sign * cc, cc], axis=1)

    cos = expand(np.cos(ang_r), np.cos(ang_c), 1.0)
    sin = expand(np.sin(ang_r), np.sin(ang_c), -1.0)
    cos = jnp.concatenate([cos, jnp.ones((C, HEAD_DIM), jnp.float32)], axis=0)
    sin = jnp.concatenate([sin, jnp.zeros((C, HEAD_DIM), jnp.float32)], axis=0)
    return cos, sin


def kernel(x, c, ctx, c_ctx, ada_w, ada_b, norm_mix, norm_ffn, attn_w_qkv, attn_q_norm, attn_k_norm,
           attn_w_o, pool_w, pool_scale, moe_router_w, moe_router_b, moe_w_gu, moe_b_gu, moe_w_down,
           moe_b_down, final_norm):
    B, S, D = x.shape
    C = ctx.shape[1]
    assert B == 1 and D == D_MODEL and S % GRID_W == 0
    assert S % TM == 0 and C % TM == 0 and S % RM == 0 and S % AQ == 0 and (S * TOP_K) % EM == 0
    x2d = x.reshape(S, D)
    ctx2d = ctx.reshape(C, D)
    bf = jnp.bfloat16

    cvec = _pad_rows([c.reshape(-1), c_ctx])
    mod = _modulation(cvec, ada_w, ada_b)
    m_l = [mod[l, 0].reshape(N_MOD, D) for l in range(2)]
    m_c0 = mod[0, 1].reshape(N_MOD, D)

    def split_router(w):
        w = jnp.pad(w, ((0, 0), (0, LANES - N_EXPERTS)))
        w1 = w.astype(bf)
        r1 = w - w1.astype(jnp.float32)
        w2 = r1.astype(bf)
        w3 = (r1 - w2.astype(jnp.float32)).astype(bf)
        return jnp.concatenate([w1, w2, w3], axis=1)

    rw = [split_router(moe_router_w[l]) for l in range(2)]
    rb = [jnp.pad(moe_router_b[l], (0, LANES - N_EXPERTS), constant_values=NEG_BIG).reshape(1, LANES)
          for l in range(2)]

    cos_t, sin_t = _rope_tables(S, C)
    qkv_vecs = _pad_rows([m_l[0][0], m_l[0][1], m_c0[0], m_c0[1], norm_mix[0]])
    q, k, v = _qkv(x2d, ctx2d, qkv_vecs, attn_w_qkv[0].astype(bf), attn_q_norm[0].reshape(1, -1),
                   attn_k_norm[0].reshape(1, -1), cos_t, sin_t)
    o = _attention(q, k, v, S)
    vecs0 = _pad_rows([m_l[0][2], m_l[0][3], m_l[0][4], norm_ffn[0]])
    x1, hf, meta, gates, counts = _post_attn(x2d, o, attn_w_o[0].astype(bf), vecs0, rw[0], rb[0])
    x2 = _moe(0, x1, hf, meta, gates, counts, moe_w_gu, moe_b_gu, moe_w_down, moe_b_down,
              _pad_rows([m_l[0][5]]), final=False)

    mvecs = _pad_rows([m_l[1][0], m_l[1][1], norm_mix[1], pool_scale[0]])
    vecs1 = _pad_rows([m_l[1][2], m_l[1][3], m_l[1][4], norm_ffn[1]])
    x3, hf, meta, gates, counts = _pool(x2, mvecs, pool_w[0].astype(bf), vecs1, rw[1], rb[1])
    out = _moe(1, x3, hf, meta, gates, counts, moe_w_gu, moe_b_gu, moe_w_down, moe_b_down,
               _pad_rows([m_l[1][5], final_norm]), final=True)
    return out.reshape(B, S, D)
```
